```python
import jax, jax.numpy as jnp
from jax import lax
import numpy as np

D_MODEL = 2048
BATCH = 8
SEQ = 2048
DEPTH = 2
DEC_BATCH = 32
DEC_SEQ = 4
PAST_LEN = 8192
PAGE_SIZE = 128

N_GROUPS = 4
GROUP_WIDTH = D_MODEL // N_GROUPS
HEAD_DIM = 128
N_HEADS = GROUP_WIDTH // HEAD_DIM
PARTS_PER_GROUP = 4
IN_WIDTH = N_GROUPS * PARTS_PER_GROUP * GROUP_WIDTH
SB_QBLOCK = 128
HG_CHUNK = 64
CONV_WIDTH = 3
MOBA_BLOCK = 256
MOBA_TOPK = 3
MOBA_QCHUNK = 16
ROPE_THETA = 500000.0
ROPE_DIM = HEAD_DIM // 4
RMS_EPS = 1e-6

kernel_name = 'stick_hgrn_conv_moba_hybrid_step'


def rmsnorm(x, g):
    xf = x.astype(jnp.float32)
    y = xf * lax.rsqrt(jnp.mean(xf * xf, axis=-1, keepdims=True) + RMS_EPS)
    return (y * g.astype(jnp.float32)).astype(x.dtype)


def partial_rope(x, pos):
    half = ROPE_DIM // 2
    inv_freq = ROPE_THETA ** (-jnp.arange(half, dtype=jnp.float32) / half)
    ang = pos.astype(jnp.float32)[:, None] * inv_freq[None, :]
    cos = jnp.cos(ang)[None, :, None, :]
    sin = jnp.sin(ang)[None, :, None, :]
    x1 = x[..., :half].astype(jnp.float32)
    x2 = x[..., half:ROPE_DIM].astype(jnp.float32)
    rot = jnp.concatenate([x1 * cos - x2 * sin, x2 * cos + x1 * sin], axis=-1).astype(x.dtype)
    return jnp.concatenate([rot, x[..., ROPE_DIM:]], axis=-1)


def sweep_queries(fn, q, q_pos, block):
    b, t, h, d = q.shape
    if t <= block or t % block:
        return fn(q, q_pos)
    nb = t // block
    qb = q.reshape(b, nb, block, h, d).swapaxes(0, 1)
    pb = q_pos.reshape(nb, block)
    ob = lax.map(lambda a: fn(a[0], a[1]), (qb, pb))
    return ob.swapaxes(0, 1).reshape(b, t, h, ob.shape[-1])


def stick_breaking_block(q, q_pos, k, v):
    tk = k.shape[1]
    z = jnp.einsum('bqhd,bkhd->bhqk', q.astype(jnp.float32), k.astype(jnp.float32)) * HEAD_DIM ** -0.5
    mask = (jnp.arange(tk)[None, :] < q_pos[:, None])[None, None]
    log_keep = jnp.where(mask, jax.nn.log_sigmoid(-z), 0.0)
    between = lax.cumsum(log_keep, axis=3, reverse=True) - log_keep
    w = jnp.where(mask, jnp.exp(jax.nn.log_sigmoid(z) + between), 0.0)
    return jnp.einsum('bhqk,bkhd->bqhd', w, v.astype(jnp.float32)).astype(q.dtype)


def moba_prepare(k, v):
    b, tk, h, d = k.shape
    nb = -(-tk // MOBA_BLOCK)
    pad = nb * MOBA_BLOCK - tk

    def blocks(a):
        a = jnp.pad(a, ((0, 0), (0, pad), (0, 0), (0, 0)))
        return a.reshape(b, nb, MOBA_BLOCK, h, d).transpose(0, 3, 1, 2, 4)

    kb = blocks(k)
    vb = blocks(v)
    kmean = jnp.mean(kb.astype(jnp.float32), axis=3)
    return kb, vb, kmean


def moba_block(q, q_pos, kb, vb, kmean):
    b, h, nb = kmean.shape[:3]
    tq = q.shape[1]
    qf = q.astype(jnp.float32)
    own = q_pos // MOBA_BLOCK
    gate = jnp.einsum('bqhd,bhnd->bhqn', qf, kmean)
    past_block = (jnp.arange(nb)[None, :] < own[:, None])[None, None]
    gate = jnp.where(past_block, gate, -jnp.inf)
    if nb < MOBA_TOPK:
        gate = jnp.pad(gate, ((0, 0), (0, 0), (0, 0), (0, MOBA_TOPK - nb)), constant_values=-jnp.inf)
    top_val, top_idx = lax.top_k(gate, MOBA_TOPK)
    sel = jnp.concatenate([jnp.minimum(top_idx, nb - 1),
                           jnp.broadcast_to(own[None, None, :, None], (b, h, tq, 1)).astype(top_idx.dtype)], axis=-1)
    blk_ok = jnp.concatenate([jnp.isfinite(top_val), jnp.ones((b, h, tq, 1), dtype=bool)], axis=-1)
    bi = jnp.arange(b)[:, None, None, None]
    hi = jnp.arange(h)[None, :, None, None]
    kg = kb[bi, hi, sel]
    vg = vb[bi, hi, sel]
    key_pos = sel[..., None] * MOBA_BLOCK + jnp.arange(MOBA_BLOCK)
    valid = blk_ok[..., None] & (key_pos <= q_pos[None, None, :, None, None])
    logits = jnp.einsum('bqhd,bhqjkd->bhqjk', qf, kg.astype(jnp.float32)) * HEAD_DIM ** -0.5
    logits = jnp.where(valid, logits, -jnp.inf).reshape(b, h, tq, -1)
    p = jax.nn.softmax(logits, axis=-1).reshape(valid.shape)
    return jnp.einsum('bhqjk,bhqjkd->bqhd', p, vg.astype(jnp.float32)).astype(q.dtype)


def hgrn2_recurrence(q, log_f, k, v, s0):
    b, t, h, dk = q.shape
    dv = v.shape[-1]
    c = HG_CHUNK if t % HG_CHUNK == 0 else t
    nc = t // c

    def chunks(a):
        return a.astype(jnp.float32).reshape(b, nc, c, h, a.shape[-1]).transpose(1, 0, 3, 2, 4)

    causal = jnp.tril(jnp.ones((c, c), dtype=bool))[None, None, :, :, None]

    def step(s, inp):
        qc, gc, kc, vc = inp
        g_cum = jnp.cumsum(gc, axis=2)
        diff = g_cum[:, :, :, None, :] - g_cum[:, :, None, :, :]
        decay = jnp.exp(jnp.where(causal, diff, -jnp.inf))
        scores = jnp.einsum('bhtk,bhsk,bhtsk->bhts', qc, kc, decay)
        o = jnp.einsum('bhts,bhsv->bhtv', scores, vc) + jnp.einsum('bhtk,bhkv->bhtv', qc * jnp.exp(g_cum), s)
        g_last = g_cum[:, :, -1:, :]
        s = jnp.exp(g_last[:, :, 0, :])[..., None] * s + jnp.einsum('bhsk,bhsv->bhkv', kc * jnp.exp(g_last - g_cum), vc)
        return s, o

    s_fin, o = lax.scan(step, s0.astype(jnp.float32),
                        (chunks(q) * dk ** -0.5, chunks(log_f), chunks(k), chunks(v)))
    o = o.transpose(1, 0, 3, 2, 4).reshape(b, t, h, dv)
    return o, s_fin


def short_conv(z, prev, w):
    t = z.shape[1]
    zz = jnp.concatenate([prev.astype(z.dtype), z], axis=1)
    y = zz[:, 0:t] * w[0]
    for j in range(1, CONV_WIDTH):
        y = y + zz[:, j:j + t] * w[j]
    return y, zz[:, t:]


def gather_pages(pool, page_table):
    rows = pool[page_table]
    return rows.reshape(page_table.shape[0], page_table.shape[1] * pool.shape[1], pool.shape[2], pool.shape[3])


def with_past(past, new):
    return new if past is None else jnp.concatenate([past.astype(new.dtype), new], axis=1)


def mixer_layer(x, pos, past_sb_k, past_sb_v, past_mb_k, past_mb_v, hg_state, conv_state,
                w_in, w_out, g_pre, g_post, g_hg, w_conv, lb):
    b, t, _ = x.shape
    h = rmsnorm(x, g_pre)
    proj = jnp.einsum('btd,de->bte', h, w_in)
    (sb_q, sb_k, sb_v, sb_g, hg_q, hg_f, hg_i, hg_g,
     cv_u, cv_b, cv_c, cv_g, mb_q, mb_k, mb_v, mb_g) = jnp.split(proj, N_GROUPS * PARTS_PER_GROUP, axis=-1)
    heads = lambda a: a.reshape(b, t, N_HEADS, HEAD_DIM)
    flat = lambda a: a.reshape(b, t, GROUP_WIDTH)

    sb_k_new = heads(sb_k)
    sb_v_new = heads(sb_v)
    sk_all = with_past(past_sb_k, sb_k_new)
    sv_all = with_past(past_sb_v, sb_v_new)
    o_sb = sweep_queries(lambda qq, pp: stick_breaking_block(qq, pp, sk_all, sv_all), heads(sb_q), pos, SB_QBLOCK)
    o_sb = flat(o_sb) * jax.nn.silu(sb_g)

    log_f = jnp.logaddexp(jnp.log(lb), jnp.log1p(-lb) + jax.nn.log_sigmoid(hg_f.astype(jnp.float32)))
    k_hg = -jnp.expm1(log_f)
    o_hg, hg_new = hgrn2_recurrence(heads(hg_q), heads(log_f), heads(k_hg), heads(hg_i), hg_state)
    o_hg = flat(rmsnorm(o_hg, g_hg.reshape(N_HEADS, HEAD_DIM))).astype(h.dtype) * jax.nn.silu(hg_g)

    y_cv, conv_new = short_conv(cv_c * cv_u, conv_state, w_conv)
    o_cv = cv_b * y_cv * jax.nn.silu(cv_g)

    mb_q_r = partial_rope(heads(mb_q), pos)
    mb_k_new = partial_rope(heads(mb_k), pos)
    mb_v_new = heads(mb_v)
    kb, vb, kmean = moba_prepare(with_past(past_mb_k, mb_k_new), with_past(past_mb_v, mb_v_new))
    o_mb = sweep_queries(lambda qq, pp: moba_block(qq, pp, kb, vb, kmean), mb_q_r, pos, MOBA_QCHUNK)
    o_mb = flat(o_mb) * jax.nn.silu(mb_g)

    mixed = jnp.concatenate([o_sb, o_hg.astype(o_sb.dtype), o_cv.astype(o_sb.dtype), o_mb], axis=-1)
    x_out = x + rmsnorm(jnp.einsum('bte,ed->btd', mixed, w_out), g_post)
    return x_out, sb_k_new, sb_v_new, mb_k_new, mb_v_new, hg_new, conv_new


def setup_inputs(seed: int = 0) -> dict:
    key = jax.random.key(seed)
    ks = jax.random.split(key, 16)
    n_pages = PAST_LEN // PAGE_SIZE
    n_used = DEC_BATCH * n_pages
    n_phys = (5 * n_used + 3) // 4
    pool_shape = (DEPTH, n_phys, PAGE_SIZE, N_HEADS, HEAD_DIM)
    page_table = jax.random.permutation(ks[0], n_phys)[:n_used].reshape(DEC_BATCH, n_pages).astype(jnp.int32)
    return {
        'x_prompt': jax.random.normal(ks[1], (BATCH, SEQ, D_MODEL), jnp.float32),
        'x_sample': jax.random.normal(ks[2], (DEC_BATCH, DEC_SEQ, D_MODEL), jnp.float32),
        'cache_sb_k': jax.random.normal(ks[3], pool_shape, jnp.float32),
        'cache_sb_v': jax.random.normal(ks[4], pool_shape, jnp.float32),
        'cache_moba_k': jax.random.normal(ks[5], pool_shape, jnp.float32),
        'cache_moba_v': jax.random.normal(ks[6], pool_shape, jnp.float32),
        'state_hgrn': 0.5 * jax.random.normal(ks[7], (DEPTH, DEC_BATCH, N_HEADS, HEAD_DIM, HEAD_DIM), jnp.float32),
        'state_conv': jax.random.normal(ks[8], (DEPTH, DEC_BATCH, CONV_WIDTH - 1, GROUP_WIDTH), jnp.float32),
        'page_table': page_table,
        'w_in': jax.random.normal(ks[9], (DEPTH, D_MODEL, IN_WIDTH), jnp.float32) * D_MODEL ** -0.5,
        'w_out': jax.random.normal(ks[10], (DEPTH, N_GROUPS * GROUP_WIDTH, D_MODEL), jnp.float32) * (N_GROUPS * GROUP_WIDTH) ** -0.5,
        'norm_pre': 1.0 + 0.05 * jax.random.normal(ks[11], (DEPTH, D_MODEL), jnp.float32),
        'norm_post': 1.0 + 0.05 * jax.random.normal(ks[12], (DEPTH, D_MODEL), jnp.float32),
        'hgrn_out_norm': 1.0 + 0.05 * jax.random.normal(ks[13], (DEPTH, GROUP_WIDTH), jnp.float32),
        'conv_w': jax.random.normal(ks[14], (DEPTH, CONV_WIDTH, GROUP_WIDTH), jnp.float32) * CONV_WIDTH ** -0.5,
        'hgrn_lb_logits': 0.5 * jax.random.normal(ks[15], (DEPTH, GROUP_WIDTH), jnp.float32),
    }


def reference(x_prompt, x_sample, cache_sb_k, cache_sb_v, cache_moba_k, cache_moba_v, state_hgrn, state_conv,
              page_table, w_in, w_out, norm_pre, norm_post, hgrn_out_norm, conv_w, hgrn_lb_logits):
    p_lb = jax.nn.softmax(hgrn_lb_logits.astype(jnp.float32), axis=0)
    lower_bounds = jnp.maximum(jnp.cumsum(p_lb, axis=0) - p_lb[0:1], 0.0)
    b_p, t_p = x_prompt.shape[0], x_prompt.shape[1]
    pos_p = jnp.arange(t_p, dtype=jnp.int32)
    pos_s = PAST_LEN + jnp.arange(x_sample.shape[1], dtype=jnp.int32)
    xp, xs = x_prompt, x_sample
    outs_p = [[] for _ in range(6)]
    outs_s = [[] for _ in range(6)]
    for l in range(DEPTH):
        weights = (w_in[l], w_out[l], norm_pre[l], norm_post[l], hgrn_out_norm[l], conv_w[l], lower_bounds[l])
        hg0 = jnp.zeros((b_p, N_HEADS, HEAD_DIM, HEAD_DIM), jnp.float32)
        cv0 = jnp.zeros((b_p, CONV_WIDTH - 1, GROUP_WIDTH), xp.dtype)
        xp, *st_p = mixer_layer(xp, pos_p, None, None, None, None, hg0, cv0, *weights)
        xs, *st_s = mixer_layer(xs, pos_s,
                                gather_pages(cache_sb_k[l], page_table), gather_pages(cache_sb_v[l], page_table),
                                gather_pages(cache_moba_k[l], page_table), gather_pages(cache_moba_v[l], page_table),
                                state_hgrn[l], state_conv[l], *weights)
        for lst, a in zip(outs_p, st_p):
            lst.append(a)
        for lst, a in zip(outs_s, st_s):
            lst.append(a)
    sb_k_prompt = jnp.stack(outs_p[0])
    sb_v_prompt = jnp.stack(outs_p[1])
    moba_k_prompt = jnp.stack(outs_p[2])
    moba_v_prompt = jnp.stack(outs_p[3])
    hgrn_prompt = jnp.stack(outs_p[4])
    conv_prompt = jnp.stack(outs_p[5])
    sb_k_sample = jnp.stack(outs_s[0])
    sb_v_sample = jnp.stack(outs_s[1])
    moba_k_sample = jnp.stack(outs_s[2])
    moba_v_sample = jnp.stack(outs_s[3])
    hgrn_sample = jnp.stack(outs_s[4])
    conv_sample = jnp.stack(outs_s[5])
    return (xp, xs,
            sb_k_prompt, sb_v_prompt, moba_k_prompt, moba_v_prompt, hgrn_prompt, conv_prompt,
            sb_k_sample, sb_v_sample, moba_k_sample, moba_v_sample, hgrn_sample, conv_sample)
```

```python
import functools

import jax
import jax.numpy as jnp
from jax import lax
from jax.experimental import pallas as pl
from jax.experimental.pallas import tpu as pltpu

F32 = jnp.float32
BF16 = jnp.bfloat16

HEAD_DIM = 128
N_HEADS = 4
GROUP_WIDTH = N_HEADS * HEAD_DIM
N_PARTS = 16
CONV_WIDTH = 3
MOBA_BLOCK = 256
MOBA_TOPK = 3
ROPE_THETA = 500000.0
ROPE_HALF = HEAD_DIM // 8
RMS_EPS = 1e-6
SCALE = HEAD_DIM ** -0.5
NEG_INF = float("-inf")

(P_SB_Q, P_SB_K, P_SB_V, P_SB_G, P_HG_Q, P_HG_F, P_HG_I, P_HG_G,
 P_CV_U, P_CV_B, P_CV_C, P_CV_G, P_MB_Q, P_MB_K, P_MB_V, P_MB_G) = range(N_PARTS)

VMEM_LIMIT = 48 * 1024 * 1024


def _cparams(sem):
    return pltpu.CompilerParams(dimension_semantics=sem, vmem_limit_bytes=VMEM_LIMIT)


def _softplus(z):
    return jnp.maximum(z, 0.0) + jnp.log1p(jnp.exp(-jnp.abs(z)))


def _silu(g):
    return g / (1.0 + jnp.exp(-g))


def _split2(x):
    hi = x.astype(BF16)
    lo = (x - hi.astype(F32)).astype(BF16)
    return hi, lo


def _split3(x):
    hi = x.astype(BF16)
    r = x - hi.astype(F32)
    mid = r.astype(BF16)
    lo = (r - mid.astype(F32)).astype(BF16)
    return hi, mid, lo


def _dot(a, b):
    return jnp.dot(a, b, preferred_element_type=F32)


def _dot_nt(a, b):
    return lax.dot_general(a, b, (((1,), (1,)), ((), ())), preferred_element_type=F32)


def _dot_tn(a, b):
    return lax.dot_general(a, b, (((0,), (0,)), ((), ())), preferred_element_type=F32)


def _proj_kernel(x_ref, g_ref, w_ref, c_ref, s1_ref, s2_ref, o_ref, h_ref, *, tm):
    n = pl.program_id(1)

    @pl.when(n == 0)
    def _():
        rows = min(tm, 128)

        def body(i, _):
            r0 = pl.multiple_of(i * rows, rows)
            x = x_ref[pl.ds(r0, rows), :]
            ms = jnp.mean(x * x, axis=-1, keepdims=True)
            h = x * lax.rsqrt(ms + RMS_EPS) * g_ref[...]
            h_ref[pl.ds(r0, rows), :] = h.astype(BF16)
            return 0

        lax.fori_loop(0, tm // rows, body, 0)

    acc = _dot(h_ref[...], w_ref[...])
    is_rope = jnp.logical_or(n == P_MB_Q, n == P_MB_K)

    @pl.when(is_rope)
    def _():
        for hh in range(N_HEADS):
            a = acc[:, hh * HEAD_DIM:(hh + 1) * HEAD_DIM]
            rot = (a * c_ref[...]
                   + pltpu.roll(a, HEAD_DIM - ROPE_HALF, 1) * s1_ref[...]
                   + pltpu.roll(a, ROPE_HALF, 1) * s2_ref[...])
            o_ref[0, :, hh * HEAD_DIM:(hh + 1) * HEAD_DIM] = rot

    @pl.when(jnp.logical_not(is_rope))
    def _():
        o_ref[0] = acc


def _rope_tables(pos):
    inv_freq = ROPE_THETA ** (-jnp.arange(ROPE_HALF, dtype=F32) / ROPE_HALF)
    ang = pos.astype(F32)[:, None] * inv_freq[None, :]
    cos, sin = jnp.cos(ang), jnp.sin(ang)
    n = pos.shape[0]
    rest = HEAD_DIM - 2 * ROPE_HALF
    c = jnp.concatenate([cos, cos, jnp.ones((n, rest), F32)], axis=1)
    s1 = jnp.concatenate([-sin, jnp.zeros((n, HEAD_DIM - ROPE_HALF), F32)], axis=1)
    s2 = jnp.concatenate([jnp.zeros((n, ROPE_HALF), F32), sin, jnp.zeros((n, rest), F32)], axis=1)
    return c, s1, s2


def _project(x, g_pre, w_in, tables, tm):
    m, d = x.shape
    period_blocks = tables[0].shape[0] // tm
    tab_spec = pl.BlockSpec((tm, HEAD_DIM), lambda i, n: (i % period_blocks, 0))
    return pl.pallas_call(
        functools.partial(_proj_kernel, tm=tm),
        grid=(m // tm, N_PARTS),
        in_specs=[
            pl.BlockSpec((tm, d), lambda i, n: (i, 0)),
            pl.BlockSpec((1, d), lambda i, n: (0, 0)),
            pl.BlockSpec((d, GROUP_WIDTH), lambda i, n: (0, n)),
            tab_spec, tab_spec, tab_spec,
        ],
        out_specs=pl.BlockSpec((1, tm, GROUP_WIDTH), lambda i, n: (n, i, 0)),
        out_shape=jax.ShapeDtypeStruct((N_PARTS, m, GROUP_WIDTH), F32),
        scratch_shapes=[pltpu.VMEM((tm, d), BF16)],
        compiler_params=_cparams(("parallel", "arbitrary")),
        name="in_proj",
    )(x, g_pre.reshape(1, d), w_in, *tables)


def _out_kernel(a_ref, b_ref, c_ref, d_ref, w_ref, x_ref, g_ref, o_ref):
    acc = _dot(a_ref[...], w_ref[0])
    acc += _dot(b_ref[...], w_ref[1])
    acc += _dot(c_ref[...], w_ref[2])
    acc += _dot(d_ref[...], w_ref[3])
    ms = jnp.mean(acc * acc, axis=-1, keepdims=True)
    o_ref[...] = x_ref[...] + acc * lax.rsqrt(ms + RMS_EPS) * g_ref[...]


def _out_project(mixed, w_out, x, g_post, tm):
    m, d = x.shape
    mix_spec = pl.BlockSpec((tm, GROUP_WIDTH), lambda i: (i, 0))
    return pl.pallas_call(
        _out_kernel,
        grid=(m // tm,),
        in_specs=[mix_spec, mix_spec, mix_spec, mix_spec,
                  pl.BlockSpec((4, GROUP_WIDTH, d), lambda i: (0, 0, 0)),
                  pl.BlockSpec((tm, d), lambda i: (i, 0)),
                  pl.BlockSpec((1, d), lambda i: (0, 0))],
        out_specs=pl.BlockSpec((tm, d), lambda i: (i, 0)),
        out_shape=jax.ShapeDtypeStruct((m, d), F32),
        compiler_params=_cparams(("parallel",)),
        name="out_proj",
    )(*mixed, w_out, x, g_post.reshape(1, d))


def _strict_upper_ones(n):
    r = lax.broadcasted_iota(jnp.int32, (n, n), 0)
    c = lax.broadcasted_iota(jnp.int32, (n, n), 1)
    return jnp.where(r > c, 1.0, 0.0).astype(BF16)


def _sb_prompt_kernel(q_ref, k_ref, v_ref, g_ref, o_ref, *, tq, tk):
    qi = pl.program_id(2)
    q = q_ref[0].astype(BF16)
    tri = _strict_upper_ones(tk)
    row_pos = qi * tq + lax.broadcasted_iota(jnp.int32, (tq, tk), 0)
    col = lax.broadcasted_iota(jnp.int32, (tq, tk), 1)
    nkb = (qi + 1) * (tq // tk)

    def body(i, carry):
        acc, run = carry
        kb = nkb - 1 - i
        k0 = pl.multiple_of(kb * tk, tk)
        k = k_ref[0, pl.ds(k0, tk), :].astype(BF16)
        v = v_ref[0, pl.ds(k0, tk), :].astype(BF16)
        z = _dot_nt(q, k) * SCALE
        mask = (k0 + col) < row_pos
        lk = jnp.where(mask, -_softplus(z), 0.0)
        hi, lo = _split2(lk)
        between = _dot(hi, tri) + _dot(lo, tri) + run
        w = jnp.where(mask, jnp.exp(z + lk + between), 0.0)
        acc = acc + _dot(w.astype(BF16), v)
        run = run + jnp.sum(lk, axis=-1, keepdims=True)
        return acc, run

    acc, _ = lax.fori_loop(0, nkb, body,
                           (jnp.zeros((tq, HEAD_DIM), F32), jnp.zeros((tq, 1), F32)))
    o_ref[...] = (acc * _silu(g_ref[0])).astype(o_ref.dtype)


def _sb_prompt(proj, b, t, tq=256, tk=128):
    nq = t // tq
    qspec = lambda part: pl.BlockSpec((1, tq, HEAD_DIM), lambda bi, h, i: (part, bi * nq + i, h))
    kvspec = lambda part: pl.BlockSpec((1, t, HEAD_DIM), lambda bi, h, i: (part, bi, h))
    return pl.pallas_call(
        functools.partial(_sb_prompt_kernel, tq=tq, tk=tk),
        grid=(b, N_HEADS, nq),
        in_specs=[qspec(P_SB_Q), kvspec(P_SB_K), kvspec(P_SB_V), qspec(P_SB_G)],
        out_specs=pl.BlockSpec((tq, HEAD_DIM), lambda bi, h, i: (bi * nq + i, h)),
        out_shape=jax.ShapeDtypeStruct((b * t, GROUP_WIDTH), BF16),
        compiler_params=_cparams(("parallel", "parallel", "arbitrary")),
        name="sb_prompt",
    )(proj, proj, proj, proj)


def _lower_bound_logs(lb_logits, layer):
    depth = lb_logits.shape[0]
    rows = [lb_logits[i:i + 1, :] for i in range(depth)]
    mx = functools.reduce(jnp.maximum, rows)
    es = [jnp.exp(r - mx) for r in rows]
    tot = functools.reduce(lambda a, c: a + c, es)
    ps = [e / tot for e in es]
    cs = ps[0]
    for i in range(1, layer + 1):
        cs = cs + ps[i]
    lb = jnp.maximum(cs - ps[0], 0.0)
    return jnp.log(lb), jnp.log1p(-lb)


def _log_forget(x, log_lb, log_1m_lb):
    a = log_lb
    bb = log_1m_lb - _softplus(-x)
    return jnp.maximum(a, bb) + jnp.log1p(jnp.exp(-jnp.abs(a - bb)))


def _col_bcast(row):
    n = row.shape[-1]
    return jnp.broadcast_to(row, (n, n)).T


def _head_rmsnorm(o, g_row):
    ms = jnp.mean(o * o, axis=-1, keepdims=True)
    return o * lax.rsqrt(ms + RMS_EPS) * g_row


def _hg_prompt_kernel(q_ref, f_ref, i_ref, g_ref, lb_ref, gn_ref, o_ref, s_out_ref, s_ref,
                      *, c, layer):
    ci = pl.program_id(2)
    nc = pl.num_programs(2)
    d = HEAD_DIM

    @pl.when(ci == 0)
    def _():
        s_ref[...] = jnp.zeros_like(s_ref)

    log_lb, log_1m_lb = _lower_bound_logs(lb_ref[...], layer)
    g = _log_forget(f_ref[0], log_lb, log_1m_lb)
    kk = 1.0 - jnp.exp(g)
    qs = q_ref[0] * SCALE
    v = i_ref[0]
    vb = v.astype(BF16)

    r = lax.broadcasted_iota(jnp.int32, (c, c), 0)
    cc = lax.broadcasted_iota(jnp.int32, (c, c), 1)
    lower = jnp.where(r >= cc, 1.0, 0.0).astype(BF16)
    g_hi, g_mid, g_lo = _split3(g)
    gc = _dot(lower, g_hi) + _dot(lower, g_mid) + _dot(lower, g_lo)
    g_last = gc[c - 1:c, :]

    s = s_ref[...]
    o = _dot((qs * jnp.exp(gc)).astype(BF16), s.astype(BF16))

    a = jnp.zeros((c, c), F32)
    blk = 8
    while blk < c:
        nb = c // blk
        gc3 = gc.reshape(nb, blk, d)
        g_end = gc3[:, blk - 1:blk, :]
        g_prev = jnp.concatenate([jnp.zeros((1, 1, d), F32), g_end[:nb - 1]], axis=0)
        qd = (qs.reshape(nb, blk, d) * jnp.exp(gc3 - g_prev)).reshape(c, d)
        kd = (kk.reshape(nb, blk, d) * jnp.exp(g_end - gc3)).reshape(c, d)
        sc = _dot_nt(qd.astype(BF16), kd.astype(BF16))
        rb = r // blk
        cb = cc // blk
        pair = jnp.logical_and(rb % 2 == 1, cb == rb - 1)
        a = jnp.where(pair, sc, a)
        blk *= 2
    o = o + _dot(a.astype(BF16), vb)

    n8 = c // 8
    gc8 = gc.reshape(n8, 8, d)
    q8 = qs.reshape(n8, 8, d)
    k8 = kk.reshape(n8, 8, d)
    v8 = v.reshape(n8, 8, d)
    sub = lax.broadcasted_iota(jnp.int32, (n8, 8, d), 1)
    od = jnp.zeros((n8, 8, d), F32)
    for j in range(8):
        diff = gc8[:, j:j + 1, :] - gc8
        e = jnp.exp(jnp.where(sub <= j, diff, NEG_INF))
        sc = jnp.sum(e * k8 * q8[:, j:j + 1, :], axis=-1, keepdims=True)
        oj = jnp.sum(sc * v8, axis=1, keepdims=True)
        od = jnp.where(sub == j, oj, od)
    o = o + od.reshape(c, d)

    kd = (kk * jnp.exp(g_last - gc)).astype(BF16)
    s_new = _col_bcast(jnp.exp(g_last)) * s + _dot_tn(kd, vb)
    s_ref[...] = s_new

    @pl.when(ci == nc - 1)
    def _():
        s_out_ref[0, 0] = s_new

    o_ref[...] = (_head_rmsnorm(o, gn_ref[...]) * _silu(g_ref[0])).astype(o_ref.dtype)


def _hg_prompt(proj, lb_logits, g_hg, layer, b, t, c=256):
    c = min(c, t)
    nc = t // c
    depth = lb_logits.shape[0]
    spec = lambda part: pl.BlockSpec((1, c, HEAD_DIM), lambda bi, h, i: (part, bi * nc + i, h))
    return pl.pallas_call(
        functools.partial(_hg_prompt_kernel, c=c, layer=layer),
        grid=(b, N_HEADS, nc),
        in_specs=[spec(P_HG_Q), spec(P_HG_F), spec(P_HG_I), spec(P_HG_G),
                  pl.BlockSpec((depth, HEAD_DIM), lambda bi, h, i: (0, h)),
                  pl.BlockSpec((1, HEAD_DIM), lambda bi, h, i: (0, h))],
        out_specs=[pl.BlockSpec((c, HEAD_DIM), lambda bi, h, i: (bi * nc + i, h)),
                   pl.BlockSpec((1, 1, HEAD_DIM, HEAD_DIM), lambda bi, h, i: (bi, h, 0, 0))],
        out_shape=[jax.ShapeDtypeStruct((b * t, GROUP_WIDTH), BF16),
                   jax.ShapeDtypeStruct((b, N_HEADS, HEAD_DIM, HEAD_DIM), F32)],
        scratch_shapes=[pltpu.VMEM((HEAD_DIM, HEAD_DIM), F32)],
        compiler_params=_cparams(("parallel", "parallel", "arbitrary")),
        name="hgrn_prompt",
    )(proj, proj, proj, proj, lb_logits, g_hg.reshape(1, GROUP_WIDTH))


def _conv_prompt_kernel(u_ref, b_ref, c_ref, g_ref, w_ref, o_ref, st_ref, carry_ref, *, tt):
    ti = pl.program_id(1)
    nt = pl.num_programs(1)

    @pl.when(ti == 0)
    def _():
        carry_ref[...] = jnp.zeros_like(carry_ref)

    z = c_ref[0] * u_ref[0]
    row = lax.broadcasted_iota(jnp.int32, z.shape, 0)
    p0 = carry_ref[0:1, :]
    p1 = carry_ref[1:2, :]
    z1 = jnp.where(row == 0, p1, pltpu.roll(z, 1, 0))
    z2 = jnp.where(row == 0, p0, jnp.where(row == 1, p1, pltpu.roll(z, 2, 0)))
    y = z2 * w_ref[0:1, :] + z1 * w_ref[1:2, :] + z * w_ref[2:3, :]
    o_ref[...] = (b_ref[0] * y * _silu(g_ref[0])).astype(o_ref.dtype)
    tail = z[tt - 2:tt, :]
    carry_ref[0:2, :] = tail

    @pl.when(ti == nt - 1)
    def _():
        st_ref[0] = tail


def _conv_prompt(proj, conv_w_l, b, t, tt=512):
    tt = min(tt, t)
    nt = t // tt
    spec = lambda part: pl.BlockSpec((1, tt, GROUP_WIDTH), lambda bi, i: (part, bi * nt + i, 0))
    return pl.pallas_call(
        functools.partial(_conv_prompt_kernel, tt=tt),
        grid=(b, nt),
        in_specs=[spec(P_CV_U), spec(P_CV_B), spec(P_CV_C), spec(P_CV_G),
                  pl.BlockSpec((CONV_WIDTH, GROUP_WIDTH), lambda bi, i: (0, 0))],
        out_specs=[pl.BlockSpec((tt, GROUP_WIDTH), lambda bi, i: (bi * nt + i, 0)),
                   pl.BlockSpec((1, CONV_WIDTH - 1, GROUP_WIDTH), lambda bi, i: (bi, 0, 0))],
        out_shape=[jax.ShapeDtypeStruct((b * t, GROUP_WIDTH), BF16),
                   jax.ShapeDtypeStruct((b, CONV_WIDTH - 1, GROUP_WIDTH), F32)],
        scratch_shapes=[pltpu.VMEM((8, GROUP_WIDTH), F32)],
        compiler_params=_cparams(("parallel", "arbitrary")),
        name="conv_prompt",
    )(proj, proj, proj, proj, conv_w_l)


def _moba_select(gate, n_past, n_blocks):
    lane = lax.broadcasted_iota(jnp.int32, gate.shape, 1)
    gm = jnp.where(lane < n_past, gate, NEG_INF)
    sel = jnp.full(gate.shape, NEG_INF, F32)
    for n in range(n_blocks):
        col = gm[:, n:n + 1]
        beats = jnp.where(gm > col, 1.0, jnp.where(jnp.logical_and(gm == col, lane < n), 1.0, 0.0))
        rank = jnp.sum(beats, axis=-1, keepdims=True)
        ok = jnp.logical_and(rank < MOBA_TOPK, col > NEG_INF)
        sel = jnp.where(jnp.logical_and(lane == n, ok), 0.0, sel)
    return sel


def _lane_pick(mat, idx):
    lane = lax.broadcasted_iota(jnp.int32, mat.shape, 1)
    return jnp.max(jnp.where(lane == idx, mat, NEG_INF), axis=-1, keepdims=True)


def _mb_prompt_kernel(q_ref, k_ref, v_ref, g_ref, o_ref, kmean_ref, *, nb):
    qi = pl.program_id(2)
    blk = MOBA_BLOCK

    @pl.when(qi == 0)
    def _():
        kmean_ref[...] = jnp.zeros_like(kmean_ref)
        for n in range(nb):
            kmean_ref[n:n + 1, :] = jnp.mean(k_ref[0, n * blk:(n + 1) * blk, :], axis=0, keepdims=True)

    q = q_ref[0]
    qb = q.astype(BF16)
    gate = lax.dot_general(q, kmean_ref[...], (((1,), (1,)), ((), ())),
                           precision=lax.Precision.HIGHEST, preferred_element_type=F32)
    sel = _moba_select(gate, qi, nb)

    def scores(j):
        k0 = pl.multiple_of(j * blk, blk)
        k = k_ref[0, pl.ds(k0, blk), :].astype(BF16)
        v = v_ref[0, pl.ds(k0, blk), :].astype(BF16)
        return _dot_nt(qb, k) * SCALE, v

    s, v = scores(qi)
    r = lax.broadcasted_iota(jnp.int32, (blk, blk), 0)
    c = lax.broadcasted_iota(jnp.int32, (blk, blk), 1)
    s = jnp.where(c <= r, s, NEG_INF)
    m = jnp.max(s, axis=-1, keepdims=True)
    p = jnp.exp(s - m)
    l = jnp.sum(p, axis=-1, keepdims=True)
    acc = _dot(p.astype(BF16), v)

    def body(j, carry):
        m, l, acc = carry
        s, v = scores(j)
        s = s + _lane_pick(sel, j)
        m_new = jnp.maximum(m, jnp.max(s, axis=-1, keepdims=True))
        alpha = jnp.exp(m - m_new)
        p = jnp.exp(s - m_new)
        l = alpha * l + jnp.sum(p, axis=-1, keepdims=True)
        acc = alpha * acc + _dot(p.astype(BF16), v)
        return m_new, l, acc

    m, l, acc = lax.fori_loop(0, qi, body, (m, l, acc))
    o_ref[...] = (acc / l * _silu(g_ref[0])).astype(o_ref.dtype)


def _mb_prompt(proj, b, t):
    blk = MOBA_BLOCK
    nb = t // blk
    qspec = lambda part: pl.BlockSpec((1, blk, HEAD_DIM), lambda bi, h, i: (part, bi * nb + i, h))
    kvspec = lambda part: pl.BlockSpec((1, t, HEAD_DIM), lambda bi, h, i: (part, bi, h))
    return pl.pallas_call(
        functools.partial(_mb_prompt_kernel, nb=nb),
        grid=(b, N_HEADS, nb),
        in_specs=[qspec(P_MB_Q), kvspec(P_MB_K), kvspec(P_MB_V), qspec(P_MB_G)],
        out_specs=pl.BlockSpec((blk, HEAD_DIM), lambda bi, h, i: (bi * nb + i, h)),
        out_shape=jax.ShapeDtypeStruct((b * t, GROUP_WIDTH), BF16),
        scratch_shapes=[pltpu.VMEM((HEAD_DIM, HEAD_DIM), F32)],
        compiler_params=_cparams(("parallel", "parallel", "arbitrary")),
        name="moba_prompt",
    )(proj, proj, proj, proj)


def _head_rows(q, t):
    rows = jnp.concatenate([q] * N_HEADS, axis=0)
    r = lax.broadcasted_iota(jnp.int32, rows.shape, 0)
    c = lax.broadcasted_iota(jnp.int32, rows.shape, 1)
    return jnp.where(c // HEAD_DIM == r // t, rows, 0.0)


def _head_fold(acc, t):
    r = lax.broadcasted_iota(jnp.int32, acc.shape, 0)
    c = lax.broadcasted_iota(jnp.int32, acc.shape, 1)
    a = jnp.where(c // HEAD_DIM == r // t, acc, 0.0)
    out = a[0:t]
    for h in range(1, N_HEADS):
        out = out + a[h * t:(h + 1) * t]
    return out


def _sb_decode_kernel(pt_ref, q_ref, kn_ref, vn_ref, g_ref, *rest, t, n_group):
    k_refs = rest[:n_group]
    v_refs = rest[n_group:2 * n_group]
    o_ref, acc_ref, run_ref, qr_ref = rest[2 * n_group:]
    s = pl.program_id(1)
    ns = pl.num_programs(1)
    ps = k_refs[0].shape[0]
    rows = N_HEADS * t

    @pl.when(s == 0)
    def _():
        qr = _head_rows(q_ref[0, 0], t)
        qr_ref[...] = qr
        tq = lax.broadcasted_iota(jnp.int32, (rows, 1), 0) % t
        kn = kn_ref[0, 0]
        vn = vn_ref[0, 0]
        acc = jnp.zeros((rows, GROUP_WIDTH), F32)
        run = jnp.zeros((rows, 1), F32)
        for j in range(t - 1, -1, -1):
            z = jnp.sum(qr * kn[j:j + 1, :], axis=-1, keepdims=True) * SCALE
            ok = tq > j
            lk = jnp.where(ok, -_softplus(z), 0.0)
            w = jnp.where(ok, jnp.exp(z + lk + run), 0.0)
            acc = acc + w * vn[j:j + 1, :]
            run = run + lk
        acc_ref[...] = acc
        run_ref[...] = jnp.broadcast_to(run, run_ref.shape)

    qb = qr_ref[...].astype(BF16)
    tri = _strict_upper_ones(ps)
    acc = acc_ref[...]
    run = run_ref[:, 0:1]
    for gi in range(n_group - 1, -1, -1):
        k = k_refs[gi][...].astype(BF16)
        v = v_refs[gi][...].astype(BF16)
        z = _dot_nt(qb, k) * SCALE
        lk = -_softplus(z)
        hi, lo = _split2(lk)
        between = _dot(hi, tri) + _dot(lo, tri) + run
        w = jnp.exp(z + lk + between)
        acc = acc + _dot(w.astype(BF16), v)
        run = run + jnp.sum(lk, axis=-1, keepdims=True)
    acc_ref[...] = acc
    run_ref[...] = jnp.broadcast_to(run, run_ref.shape)

    @pl.when(s == ns - 1)
    def _():
        o_ref[0] = _head_fold(acc, t) * _silu(g_ref[0, 0])


def _page_specs(layer, n_steps, n_group, ps, reverse):
    specs = []
    for gi in range(n_group):
        if reverse:
            imap = lambda bi, s, pt, gi=gi: (layer, pt[bi, (n_steps - 1 - s) * n_group + gi], 0, 0)
        else:
            imap = lambda bi, s, pt, gi=gi: (layer, pt[bi, s * n_group + gi], 0, 0)
        specs.append(pl.BlockSpec((None, None, ps, GROUP_WIDTH), imap))
    return specs


def _sb_decode(proj4, cache_k, cache_v, page_table, layer, n_group=8):
    _, bd, t, _ = proj4.shape
    n_pages = page_table.shape[1]
    ps = cache_k.shape[2]
    n_group = min(n_group, n_pages)
    n_steps = n_pages // n_group
    tok = lambda part: pl.BlockSpec((1, 1, t, GROUP_WIDTH), lambda bi, s, pt: (part, bi, 0, 0))
    rows = N_HEADS * t
    grid_spec = pltpu.PrefetchScalarGridSpec(
        num_scalar_prefetch=1,
        grid=(bd, n_steps),
        in_specs=[tok(P_SB_Q), tok(P_SB_K), tok(P_SB_V), tok(P_SB_G)]
        + _page_specs(layer, n_steps, n_group, ps, True)
        + _page_specs(layer, n_steps, n_group, ps, True),
        out_specs=pl.BlockSpec((1, t, GROUP_WIDTH), lambda bi, s, pt: (bi, 0, 0)),
        scratch_shapes=[pltpu.VMEM((rows, GROUP_WIDTH), F32),
                        pltpu.VMEM((rows, HEAD_DIM), F32),
                        pltpu.VMEM((rows, GROUP_WIDTH), F32)],
    )
    return pl.pallas_call(
        functools.partial(_sb_decode_kernel, t=t, n_group=n_group),
        grid_spec=grid_spec,
        out_shape=jax.ShapeDtypeStruct((bd, t, GROUP_WIDTH), F32),
        compiler_params=_cparams(("parallel", "arbitrary")),
        name="sb_decode",
    )(page_table, proj4, proj4, proj4, proj4, *([cache_k] * n_group), *([cache_v] * n_group))


def _mb_gate_kernel(pt_ref, q_ref, *rest, t, n_group):
    k_refs = rest[:n_group]
    gate_ref, logit_ref, qr_ref = rest[n_group:]
    s = pl.program_id(1)
    ps = k_refs[0].shape[0]
    per_blk = MOBA_BLOCK // ps

    @pl.when(s == 0)
    def _():
        qr_ref[...] = _head_rows(q_ref[0, 0], t)
        gate_ref[...] = jnp.full(gate_ref.shape, NEG_INF, F32)

    qr = qr_ref[...]
    qb = qr.astype(BF16)
    lane = lax.broadcasted_iota(jnp.int32, gate_ref.shape[1:], 1)
    gates = gate_ref[0]
    for bi in range(n_group // per_blk):
        ksum = jnp.zeros((1, GROUP_WIDTH), F32)
        for pi in range(per_blk):
            gi = bi * per_blk + pi
            k = k_refs[gi][...]
            ksum = ksum + jnp.sum(k, axis=0, keepdims=True)
            logit_ref[0, :, gi * ps:(gi + 1) * ps] = _dot_nt(qb, k.astype(BF16))
        col = jnp.sum(qr * (ksum * (1.0 / MOBA_BLOCK)), axis=-1, keepdims=True)
        gates = jnp.where(lane == s * (n_group // per_blk) + bi, col, gates)
    gate_ref[0] = gates


def _mb_attend_kernel(pt_ref, q_ref, kn_ref, vn_ref, g_ref, gate_ref, logit_ref, *rest,
                      t, n_group, n_past):
    v_refs = rest[:n_group]
    o_ref, acc_ref, m_ref, l_ref, sel_ref = rest[n_group:]
    s = pl.program_id(1)
    ns = pl.num_programs(1)
    ps = v_refs[0].shape[0]
    per_blk = MOBA_BLOCK // ps
    rows = N_HEADS * t

    @pl.when(s == 0)
    def _():
        sel_ref[...] = _moba_select(gate_ref[0], n_past, n_past)
        qr = _head_rows(q_ref[0, 0], t)
        tq = lax.broadcasted_iota(jnp.int32, (rows, 1), 0) % t
        kn = kn_ref[0, 0]
        vn = vn_ref[0, 0]
        zs = []
        for j in range(t):
            z = jnp.sum(qr * kn[j:j + 1, :], axis=-1, keepdims=True) * SCALE
            zs.append(jnp.where(tq >= j, z, NEG_INF))
        m = functools.reduce(jnp.maximum, zs)
        l = jnp.zeros((rows, 1), F32)
        acc = jnp.zeros((rows, GROUP_WIDTH), F32)
        for j in range(t):
            p = jnp.exp(zs[j] - m)
            l = l + p
            acc = acc + p * vn[j:j + 1, :]
        acc_ref[...] = acc
        m_ref[...] = jnp.broadcast_to(m, m_ref.shape)
        l_ref[...] = jnp.broadcast_to(l, l_ref.shape)

    acc = acc_ref[...]
    m = m_ref[:, 0:1]
    l = l_ref[:, 0:1]
    sel = sel_ref[...]
    for bi in range(n_group // per_blk):
        bias = _lane_pick(sel, s * (n_group // per_blk) + bi)
        for pi in range(per_blk):
            gi = bi * per_blk + pi
            sc = logit_ref[0, :, gi * ps:(gi + 1) * ps] * SCALE + bias
            m_new = jnp.maximum(m, jnp.max(sc, axis=-1, keepdims=True))
            alpha = jnp.exp(m - m_new)
            p = jnp.exp(sc - m_new)
            l = alpha * l + jnp.sum(p, axis=-1, keepdims=True)
            acc = alpha * acc + _dot(p.astype(BF16), v_refs[gi][...].astype(BF16))
            m = m_new
    acc_ref[...] = acc
    m_ref[...] = jnp.broadcast_to(m, m_ref.shape)
    l_ref[...] = jnp.broadcast_to(l, l_ref.shape)

    @pl.when(s == ns - 1)
    def _():
        o_ref[0] = _head_fold(acc / l, t) * _silu(g_ref[0, 0])


def _mb_decode(proj4, cache_k, cache_v, page_table, layer, n_group=8):
    _, bd, t, _ = proj4.shape
    n_pages = page_table.shape[1]
    ps = cache_k.shape[2]
    n_group = min(n_group, n_pages)
    n_steps = n_pages // n_group
    n_past = n_pages * ps // MOBA_BLOCK
    rows = N_HEADS * t
    tok = lambda part: pl.BlockSpec((1, 1, t, GROUP_WIDTH), lambda bi, s, pt: (part, bi, 0, 0))
    gate_spec = pl.BlockSpec((1, rows, HEAD_DIM), lambda bi, s, pt: (bi, 0, 0))
    logit_spec = pl.BlockSpec((1, rows, n_group * ps), lambda bi, s, pt: (bi, 0, s))

    gates, logits = pl.pallas_call(
        functools.partial(_mb_gate_kernel, t=t, n_group=n_group),
        grid_spec=pltpu.PrefetchScalarGridSpec(
            num_scalar_prefetch=1,
            grid=(bd, n_steps),
            in_specs=[tok(P_MB_Q)] + _page_specs(layer, n_steps, n_group, ps, False),
            out_specs=[gate_spec, logit_spec],
            scratch_shapes=[pltpu.VMEM((rows, GROUP_WIDTH), F32)],
        ),
        out_shape=[jax.ShapeDtypeStruct((bd, rows, HEAD_DIM), F32),
                   jax.ShapeDtypeStruct((bd, rows, n_pages * ps), F32)],
        compiler_params=_cparams(("parallel", "arbitrary")),
        name="moba_decode_gate",
    )(page_table, proj4, *([cache_k] * n_group))

    return pl.pallas_call(
        functools.partial(_mb_attend_kernel, t=t, n_group=n_group, n_past=n_past),
        grid_spec=pltpu.PrefetchScalarGridSpec(
            num_scalar_prefetch=1,
            grid=(bd, n_steps),
            in_specs=[tok(P_MB_Q), tok(P_MB_K), tok(P_MB_V), tok(P_MB_G), gate_spec, logit_spec]
            + _page_specs(layer, n_steps, n_group, ps, False),
            out_specs=pl.BlockSpec((1, t, GROUP_WIDTH), lambda bi, s, pt: (bi, 0, 0)),
            scratch_shapes=[pltpu.VMEM((rows, GROUP_WIDTH), F32),
                            pltpu.VMEM((rows, HEAD_DIM), F32),
                            pltpu.VMEM((rows, HEAD_DIM), F32),
                            pltpu.VMEM((rows, HEAD_DIM), F32)],
        ),
        out_shape=jax.ShapeDtypeStruct((bd, t, GROUP_WIDTH), F32),
        compiler_params=_cparams(("parallel", "arbitrary")),
        name="moba_decode_attend",
    )(page_table, proj4, proj4, proj4, proj4, gates, logits, *([cache_v] * n_group))


def _small_decode_kernel(hq_ref, hf_ref, hi_ref, hgate_ref, cu_ref, cb_ref, cc_ref, cg_ref,
                         s0_ref, cv0_ref, lb_ref, gn_ref, w_ref,
                         ohg_ref, s_out_ref, ocv_ref, cv_out_ref, *, t, layer):
    d = HEAD_DIM
    log_lb, log_1m_lb = _lower_bound_logs(lb_ref[...], layer)
    g_all = _log_forget(hf_ref[0, 0], log_lb, log_1m_lb)
    k_all = 1.0 - jnp.exp(g_all)
    q_all = hq_ref[0, 0] * SCALE
    v_all = hi_ref[0, 0]
    gate_all = hgate_ref[0, 0]
    gn = gn_ref[...]
    pad = jnp.zeros((8 - t % 8, d), F32) if t % 8 else None

    for h in range(N_HEADS):
        sl = slice(h * d, (h + 1) * d)
        g, kk, qs, v = g_all[:, sl], k_all[:, sl], q_all[:, sl], v_all[:, sl]
        s0 = s0_ref[0, h]
        gcs = []
        run = jnp.zeros((1, d), F32)
        for i in range(t):
            run = run + g[i:i + 1]
            gcs.append(run)
        qe = jnp.concatenate([qs[i:i + 1] * jnp.exp(gcs[i]) for i in range(t)], axis=0)
        if pad is not None:
            qe = jnp.concatenate([qe, pad], axis=0)
        inter = _dot(qe.astype(BF16), s0.astype(BF16))
        s_new = _col_bcast(jnp.exp(gcs[t - 1])) * s0
        for j in range(t):
            s_new = s_new + _col_bcast(kk[j:j + 1] * jnp.exp(gcs[t - 1] - gcs[j])) * v[j:j + 1]
        s_out_ref[0, h] = s_new
        for i in range(t):
            o = inter[i:i + 1]
            for j in range(i + 1):
                sc = jnp.sum(qs[i:i + 1] * kk[j:j + 1] * jnp.exp(gcs[i] - gcs[j]),
                             axis=-1, keepdims=True)
                o = o + sc * v[j:j + 1]
            o = _head_rmsnorm(o, gn[:, sl]) * _silu(gate_all[i:i + 1, sl])
            ohg_ref[0, i:i + 1, sl] = o

    z = cc_ref[0, 0] * cu_ref[0, 0]
    zz = [cv0_ref[0, j:j + 1, :] for j in range(CONV_WIDTH - 1)] + [z[i:i + 1] for i in range(t)]
    cb = cb_ref[0, 0]
    cg = cg_ref[0, 0]
    for i in range(t):
        y = zz[i] * w_ref[0:1, :]
        for j in range(1, CONV_WIDTH):
            y = y + zz[i + j] * w_ref[j:j + 1, :]
        ocv_ref[0, i:i + 1, :] = cb[i:i + 1] * y * _silu(cg[i:i + 1])
    for j in range(CONV_WIDTH - 1):
        cv_out_ref[0, j:j + 1, :] = zz[t + j]


def _small_decode(proj4, state_hgrn, state_conv, lb_logits, g_hg, conv_w, layer):
    _, bd, t, _ = proj4.shape
    depth = lb_logits.shape[0]
    d = HEAD_DIM
    tok = lambda part: pl.BlockSpec((1, 1, t, GROUP_WIDTH), lambda bi: (part, bi, 0, 0))
    return pl.pallas_call(
        functools.partial(_small_decode_kernel, t=t, layer=layer),
        grid=(bd,),
        in_specs=[tok(P_HG_Q), tok(P_HG_F), tok(P_HG_I), tok(P_HG_G),
                  tok(P_CV_U), tok(P_CV_B), tok(P_CV_C), tok(P_CV_G),
                  pl.BlockSpec((None, 1, N_HEADS, d, d), lambda bi: (layer, bi, 0, 0, 0)),
                  pl.BlockSpec((None, 1, CONV_WIDTH - 1, GROUP_WIDTH), lambda bi: (layer, bi, 0, 0)),
                  pl.BlockSpec((depth, GROUP_WIDTH), lambda bi: (0, 0)),
                  pl.BlockSpec((1, GROUP_WIDTH), lambda bi: (0, 0)),
                  pl.BlockSpec((None, CONV_WIDTH, GROUP_WIDTH), lambda bi: (layer, 0, 0))],
        out_specs=[pl.BlockSpec((1, t, GROUP_WIDTH), lambda bi: (bi, 0, 0)),
                   pl.BlockSpec((1, N_HEADS, d, d), lambda bi: (bi, 0, 0, 0)),
                   pl.BlockSpec((1, t, GROUP_WIDTH), lambda bi: (bi, 0, 0)),
                   pl.BlockSpec((1, CONV_WIDTH - 1, GROUP_WIDTH), lambda bi: (bi, 0, 0))],
        out_shape=[jax.ShapeDtypeStruct((bd, t, GROUP_WIDTH), F32),
                   jax.ShapeDtypeStruct((bd, N_HEADS, d, d), F32),
                   jax.ShapeDtypeStruct((bd, t, GROUP_WIDTH), F32),
                   jax.ShapeDtypeStruct((bd, CONV_WIDTH - 1, GROUP_WIDTH), F32)],
        compiler_params=_cparams(("parallel",)),
        name="hgrn_conv_decode",
    )(*([proj4] * 8), state_hgrn, state_conv, lb_logits, g_hg.reshape(1, GROUP_WIDTH), conv_w)


def _row_tile(m, pref):
    return pref if m % pref == 0 else m


def kernel(x_prompt, x_sample, cache_sb_k, cache_sb_v, cache_moba_k, cache_moba_v, state_hgrn, state_conv,
           page_table, w_in, w_out, norm_pre, norm_post, hgrn_out_norm, conv_w, hgrn_lb_logits):
    b, t, d = x_prompt.shape
    bd, td, _ = x_sample.shape
    depth = w_in.shape[0]
    n_phys, ps = cache_sb_k.shape[1], cache_sb_k.shape[2]
    past_len = page_table.shape[1] * ps
    assert past_len % MOBA_BLOCK == 0 and td <= MOBA_BLOCK and t % MOBA_BLOCK == 0

    mp, ms = b * t, bd * td
    tm_p, tm_s = _row_tile(mp, 512), _row_tile(ms, 512)
    tab_p = _rope_tables(jnp.arange(t, dtype=jnp.int32))
    pos_s = past_len + jnp.arange(td, dtype=jnp.int32)
    tab_s = tuple(jnp.tile(a, (bd, 1)) for a in _rope_tables(pos_s))
    assert t % tm_p == 0 or tm_p % t == 0
    if tm_p > t:
        tab_p = tuple(jnp.tile(a, (tm_p // t, 1)) for a in tab_p)

    w_in_b = w_in.astype(BF16)
    w_out_b = w_out.astype(BF16).reshape(depth, 4, GROUP_WIDTH, d)
    pool = lambda c: c.reshape(depth, n_phys, ps, GROUP_WIDTH)
    c_sb_k, c_sb_v, c_mb_k, c_mb_v = pool(cache_sb_k), pool(cache_sb_v), pool(cache_moba_k), pool(cache_moba_v)

    xp = x_prompt.reshape(mp, d)
    xs = x_sample.reshape(ms, d)
    outs_p = [[] for _ in range(6)]
    outs_s = [[] for _ in range(6)]
    heads = lambda a, n, tt: a.reshape(n, tt, N_HEADS, HEAD_DIM)
    for l in range(depth):
        proj = _project(xp, norm_pre[l], w_in_b[l], tab_p, tm_p)
        o_sb = _sb_prompt(proj, b, t)
        o_hg, hg_new = _hg_prompt(proj, hgrn_lb_logits, hgrn_out_norm[l], l, b, t)
        o_cv, cv_new = _conv_prompt(proj, conv_w[l], b, t)
        o_mb = _mb_prompt(proj, b, t)
        xp = _out_project((o_sb, o_hg, o_cv, o_mb), w_out_b[l], xp, norm_post[l], tm_p)
        for lst, a in zip(outs_p, (heads(proj[P_SB_K], b, t), heads(proj[P_SB_V], b, t),
                                   heads(proj[P_MB_K], b, t), heads(proj[P_MB_V], b, t), hg_new, cv_new)):
            lst.append(a)

        proj_s = _project(xs, norm_pre[l], w_in_b[l], tab_s, tm_s)
        proj4 = proj_s.reshape(N_PARTS, bd, td, GROUP_WIDTH)
        s_sb = _sb_decode(proj4, c_sb_k, c_sb_v, page_table, l)
        s_hg, hg_new_s, s_cv, cv_new_s = _small_decode(proj4, state_hgrn, state_conv, hgrn_lb_logits,
                                                       hgrn_out_norm[l], conv_w, l)
        s_mb = _mb_decode(proj4, c_mb_k, c_mb_v, page_table, l)
        mixed_s = tuple(a.reshape(ms, GROUP_WIDTH).astype(BF16) for a in (s_sb, s_hg, s_cv, s_mb))
        xs = _out_project(mixed_s, w_out_b[l], xs, norm_post[l], tm_s)
        for lst, a in zip(outs_s, (heads(proj_s[P_SB_K], bd, td), heads(proj_s[P_SB_V], bd, td),
                                   heads(proj_s[P_MB_K], bd, td), heads(proj_s[P_MB_V], bd, td),
                                   hg_new_s, cv_new_s)):
            lst.append(a)

    return (xp.reshape(b, t, d), xs.reshape(bd, td, d),
            *(jnp.stack(o) for o in outs_p), *(jnp.stack(o) for o in outs_s))
```

```python
import functools

import jax
import jax.numpy as jnp
from jax import lax
from jax.experimental import pallas as pl
from jax.experimental.pallas import tpu as pltpu

F32 = jnp.float32
BF16 = jnp.bfloat16

HEAD_DIM = 128
N_HEADS = 4
GROUP_WIDTH = N_HEADS * HEAD_DIM
N_PARTS = 16
CONV_WIDTH = 3
MOBA_BLOCK = 256
MOBA_TOPK = 3
ROPE_THETA = 500000.0
ROPE_HALF = HEAD_DIM // 8
RMS_EPS = 1e-6
SCALE = HEAD_DIM ** -0.5
NEG_INF = float("-inf")

(P_SB_Q, P_SB_K, P_SB_V, P_SB_G, P_HG_Q, P_HG_F, P_HG_I, P_HG_G,
 P_CV_U, P_CV_B, P_CV_C, P_CV_G, P_MB_Q, P_MB_K, P_MB_V, P_MB_G) = range(N_PARTS)

VMEM_LIMIT = 48 * 1024 * 1024


def _cparams(sem):
    return pltpu.CompilerParams(dimension_semantics=sem, vmem_limit_bytes=VMEM_LIMIT)


def _softplus(z):
    return jnp.maximum(z, 0.0) + jnp.log1p(jnp.exp(-jnp.abs(z)))


def _silu(g):
    return g / (1.0 + jnp.exp(-g))


def _split2(x):
    hi = x.astype(BF16)
    lo = (x - hi.astype(F32)).astype(BF16)
    return hi, lo


def _split3(x):
    hi = x.astype(BF16)
    r = x - hi.astype(F32)
    mid = r.astype(BF16)
    lo = (r - mid.astype(F32)).astype(BF16)
    return hi, mid, lo


def _dot(a, b):
    return jnp.dot(a, b, preferred_element_type=F32)


def _dot_nt(a, b):
    return lax.dot_general(a, b, (((1,), (1,)), ((), ())), preferred_element_type=F32)


def _dot_tn(a, b):
    return lax.dot_general(a, b, (((0,), (0,)), ((), ())), preferred_element_type=F32)


def _proj_kernel(x_ref, g_ref, w_ref, c_ref, s1_ref, s2_ref, o_ref, h_ref, *, tm):
    n = pl.program_id(1)

    @pl.when(n == 0)
    def _():
        rows = min(tm, 128)

        def body(i, _):
            r0 = pl.multiple_of(i * rows, rows)
            x = x_ref[pl.ds(r0, rows), :]
            ms = jnp.mean(x * x, axis=-1, keepdims=True)
            h = x * lax.rsqrt(ms + RMS_EPS) * g_ref[...]
            h_ref[pl.ds(r0, rows), :] = h.astype(BF16)
            return 0

        lax.fori_loop(0, tm // rows, body, 0)

    acc = _dot(h_ref[...], w_ref[...])
    is_rope = jnp.logical_or(n == P_MB_Q, n == P_MB_K)

    @pl.when(is_rope)
    def _():
        for hh in range(N_HEADS):
            a = acc[:, hh * HEAD_DIM:(hh + 1) * HEAD_DIM]
            rot = (a * c_ref[...]
                   + pltpu.roll(a, HEAD_DIM - ROPE_HALF, 1) * s1_ref[...]
                   + pltpu.roll(a, ROPE_HALF, 1) * s2_ref[...])
            o_ref[0, :, hh * HEAD_DIM:(hh + 1) * HEAD_DIM] = rot

    @pl.when(jnp.logical_not(is_rope))
    def _():
        o_ref[0] = acc


def _rope_tables(pos):
    inv_freq = ROPE_THETA ** (-jnp.arange(ROPE_HALF, dtype=F32) / ROPE_HALF)
    ang = pos.astype(F32)[:, None] * inv_freq[None, :]
    cos, sin = jnp.cos(ang), jnp.sin(ang)
    n = pos.shape[0]
    rest = HEAD_DIM - 2 * ROPE_HALF
    c = jnp.concatenate([cos, cos, jnp.ones((n, rest), F32)], axis=1)
    s1 = jnp.concatenate([-sin, jnp.zeros((n, HEAD_DIM - ROPE_HALF), F32)], axis=1)
    s2 = jnp.concatenate([jnp.zeros((n, ROPE_HALF), F32), sin, jnp.zeros((n, rest), F32)], axis=1)
    return c, s1, s2


def _project(x, g_pre, w_in, tables, tm):
    m, d = x.shape
    period_blocks = tables[0].shape[0] // tm
    tab_spec = pl.BlockSpec((tm, HEAD_DIM), lambda i, n: (i % period_blocks, 0))
    return pl.pallas_call(
        functools.partial(_proj_kernel, tm=tm),
        grid=(m // tm, N_PARTS),
        in_specs=[
            pl.BlockSpec((tm, d), lambda i, n: (i, 0)),
            pl.BlockSpec((1, d), lambda i, n: (0, 0)),
            pl.BlockSpec((d, GROUP_WIDTH), lambda i, n: (0, n)),
            tab_spec, tab_spec, tab_spec,
        ],
        out_specs=pl.BlockSpec((1, tm, GROUP_WIDTH), lambda i, n: (n, i, 0)),
        out_shape=jax.ShapeDtypeStruct((N_PARTS, m, GROUP_WIDTH), F32),
        scratch_shapes=[pltpu.VMEM((tm, d), BF16)],
        compiler_params=_cparams(("parallel", "arbitrary")),
        name="in_proj",
    )(x, g_pre.reshape(1, d), w_in, *tables)


def _out_kernel(a_ref, b_ref, c_ref, d_ref, w_ref, x_ref, g_ref, o_ref):
    acc = _dot(a_ref[...], w_ref[0])
    acc += _dot(b_ref[...], w_ref[1])
    acc += _dot(c_ref[...], w_ref[2])
    acc += _dot(d_ref[...], w_ref[3])
    ms = jnp.mean(acc * acc, axis=-1, keepdims=True)
    o_ref[...] = x_ref[...] + acc * lax.rsqrt(ms + RMS_EPS) * g_ref[...]


def _out_project(mixed, w_out, x, g_post, tm):
    m, d = x.shape
    mix_spec = pl.BlockSpec((tm, GROUP_WIDTH), lambda i: (i, 0))
    return pl.pallas_call(
        _out_kernel,
        grid=(m // tm,),
        in_specs=[mix_spec, mix_spec, mix_spec, mix_spec,
                  pl.BlockSpec((4, GROUP_WIDTH, d), lambda i: (0, 0, 0)),
                  pl.BlockSpec((tm, d), lambda i: (i, 0)),
                  pl.BlockSpec((1, d), lambda i: (0, 0))],
        out_specs=pl.BlockSpec((tm, d), lambda i: (i, 0)),
        out_shape=jax.ShapeDtypeStruct((m, d), F32),
        compiler_params=_cparams(("parallel",)),
        name="out_proj",
    )(*mixed, w_out, x, g_post.reshape(1, d))


def _strict_upper_ones(n):
    r = lax.broadcasted_iota(jnp.int32, (n, n), 0)
    c = lax.broadcasted_iota(jnp.int32, (n, n), 1)
    return jnp.where(r > c, 1.0, 0.0).astype(BF16)


def _reverse_cumsum_groups(lk, run, tri):
    gw = tri.shape[0]
    ng = lk.shape[1] // gw
    rows = lk.shape[0]
    groups = [lk[:, gi * gw:(gi + 1) * gw] for gi in range(ng)]
    hi, lo = _split2(jnp.concatenate(groups, axis=0))
    local = _dot(hi, tri) + _dot(lo, tri)
    pieces = [None] * ng
    for gi in range(ng - 1, -1, -1):
        pieces[gi] = local[gi * rows:(gi + 1) * rows] + run
        run = run + jnp.sum(groups[gi], axis=-1, keepdims=True)
    return jnp.concatenate(pieces, axis=1), run


def _sb_prompt_kernel(q_ref, k_ref, v_ref, g_ref, o_ref, *, tq):
    qi = pl.program_id(2)
    q = q_ref[0].astype(BF16)
    tri = _strict_upper_ones(HEAD_DIM)

    def block(kb, run, diagonal):
        k0 = pl.multiple_of(kb * tq, tq)
        k = k_ref[0, pl.ds(k0, tq), :].astype(BF16)
        v = v_ref[0, pl.ds(k0, tq), :].astype(BF16)
        z = _dot_nt(q, k) * SCALE
        lk = -_softplus(z)
        if diagonal:
            mask = (lax.broadcasted_iota(jnp.int32, (tq, tq), 1)
                    < lax.broadcasted_iota(jnp.int32, (tq, tq), 0))
            lk = jnp.where(mask, lk, 0.0)
        between, run = _reverse_cumsum_groups(lk, run, tri)
        w = jnp.exp(z + lk + between)
        if diagonal:
            w = jnp.where(mask, w, 0.0)
        return _dot(w.astype(BF16), v), run

    acc, run = block(qi, jnp.zeros((tq, 1), F32), True)

    def body(i, carry):
        acc, run = carry
        part, run = block(qi - 1 - i, run, False)
        return acc + part, run

    acc, _ = lax.fori_loop(0, qi, body, (acc, run))
    o_ref[...] = (acc * _silu(g_ref[0])).astype(o_ref.dtype)


def _sb_prompt(proj, b, t, tq=512):
    tq = min(tq, t)
    nq = t // tq
    qspec = lambda part: pl.BlockSpec((1, tq, HEAD_DIM), lambda bi, h, i: (part, bi * nq + i, h))
    kvspec = lambda part: pl.BlockSpec((1, t, HEAD_DIM), lambda bi, h, i: (part, bi, h))
    return pl.pallas_call(
        functools.partial(_sb_prompt_kernel, tq=tq),
        grid=(b, N_HEADS, nq),
        in_specs=[qspec(P_SB_Q), kvspec(P_SB_K), kvspec(P_SB_V), qspec(P_SB_G)],
        out_specs=pl.BlockSpec((tq, HEAD_DIM), lambda bi, h, i: (bi * nq + i, h)),
        out_shape=jax.ShapeDtypeStruct((b * t, GROUP_WIDTH), BF16),
        compiler_params=_cparams(("parallel", "parallel", "arbitrary")),
        name="sb_prompt",
    )(proj, proj, proj, proj)


def _lower_bound_logs(lb_logits, layer):
    depth = lb_logits.shape[0]
    rows = [lb_logits[i:i + 1, :] for i in range(depth)]
    mx = functools.reduce(jnp.maximum, rows)
    es = [jnp.exp(r - mx) for r in rows]
    tot = functools.reduce(lambda a, c: a + c, es)
    ps = [e / tot for e in es]
    cs = ps[0]
    for i in range(1, layer + 1):
        cs = cs + ps[i]
    lb = jnp.maximum(cs - ps[0], 0.0)
    return jnp.log(lb), jnp.log1p(-lb)


def _log_forget(x, log_lb, log_1m_lb):
    a = log_lb
    bb = log_1m_lb - _softplus(-x)
    return jnp.maximum(a, bb) + jnp.log1p(jnp.exp(-jnp.abs(a - bb)))


def _col_bcast(row):
    n = row.shape[-1]
    return jnp.broadcast_to(row, (n, n)).T


def _head_rmsnorm(o, g_row):
    ms = jnp.mean(o * o, axis=-1, keepdims=True)
    return o * lax.rsqrt(ms + RMS_EPS) * g_row


def _hg_prompt_kernel(q_ref, f_ref, i_ref, g_ref, lb_ref, gn_ref, o_ref, s_out_ref, s_ref,
                      *, c, layer):
    ci = pl.program_id(2)
    nc = pl.num_programs(2)
    d = HEAD_DIM

    @pl.when(ci == 0)
    def _():
        s_ref[...] = jnp.zeros_like(s_ref)

    log_lb, log_1m_lb = _lower_bound_logs(lb_ref[...], layer)
    g = _log_forget(f_ref[0], log_lb, log_1m_lb)
    kk = 1.0 - jnp.exp(g)
    qs = q_ref[0] * SCALE
    v = i_ref[0]
    vb = v.astype(BF16)

    r = lax.broadcasted_iota(jnp.int32, (c, c), 0)
    cc = lax.broadcasted_iota(jnp.int32, (c, c), 1)
    lower = jnp.where(r >= cc, 1.0, 0.0).astype(BF16)
    g_hi, g_mid, g_lo = _split3(g)
    gc = _dot(lower, g_hi) + _dot(lower, g_mid) + _dot(lower, g_lo)
    g_last = gc[c - 1:c, :]

    s = s_ref[...]
    o = _dot((qs * jnp.exp(gc)).astype(BF16), s.astype(BF16))

    a = jnp.zeros((c, c), F32)
    blk = 8
    while blk < c:
        nb = c // blk
        gc3 = gc.reshape(nb, blk, d)
        g_end = gc3[:, blk - 1:blk, :]
        g_prev = jnp.concatenate([jnp.zeros((1, 1, d), F32), g_end[:nb - 1]], axis=0)
        qd = (qs.reshape(nb, blk, d) * jnp.exp(gc3 - g_prev)).reshape(c, d)
        kd = (kk.reshape(nb, blk, d) * jnp.exp(g_end - gc3)).reshape(c, d)
        sc = _dot_nt(qd.astype(BF16), kd.astype(BF16))
        rb = r // blk
        cb = cc // blk
        pair = jnp.logical_and(rb % 2 == 1, cb == rb - 1)
        a = jnp.where(pair, sc, a)
        blk *= 2
    o = o + _dot(a.astype(BF16), vb)

    n8 = c // 8
    gc8 = gc.reshape(n8, 8, d)
    q8 = qs.reshape(n8, 8, d)
    k8 = kk.reshape(n8, 8, d)
    v8 = v.reshape(n8, 8, d)
    sub = lax.broadcasted_iota(jnp.int32, (n8, 8, d), 1)
    od = jnp.zeros((n8, 8, d), F32)
    for j in range(8):
        diff = gc8[:, j:j + 1, :] - gc8
        e = jnp.exp(jnp.where(sub <= j, diff, NEG_INF))
        sc = jnp.sum(e * k8 * q8[:, j:j + 1, :], axis=-1, keepdims=True)
        oj = jnp.sum(sc * v8, axis=1, keepdims=True)
        od = jnp.where(sub == j, oj, od)
    o = o + od.reshape(c, d)

    kd = (kk * jnp.exp(g_last - gc)).astype(BF16)
    s_new = _col_bcast(jnp.exp(g_last)) * s + _dot_tn(kd, vb)
    s_ref[...] = s_new

    @pl.when(ci == nc - 1)
    def _():
        s_out_ref[0, 0] = s_new

    o_ref[...] = (_head_rmsnorm(o, gn_ref[...]) * _silu(g_ref[0])).astype(o_ref.dtype)


def _hg_prompt(proj, lb_logits, g_hg, layer, b, t, c=256):
    c = min(c, t)
    nc = t // c
    depth = lb_logits.shape[0]
    spec = lambda part: pl.BlockSpec((1, c, HEAD_DIM), lambda bi, h, i: (part, bi * nc + i, h))
    return pl.pallas_call(
        functools.partial(_hg_prompt_kernel, c=c, layer=layer),
        grid=(b, N_HEADS, nc),
        in_specs=[spec(P_HG_Q), spec(P_HG_F), spec(P_HG_I), spec(P_HG_G),
                  pl.BlockSpec((depth, HEAD_DIM), lambda bi, h, i: (0, h)),
                  pl.BlockSpec((1, HEAD_DIM), lambda bi, h, i: (0, h))],
        out_specs=[pl.BlockSpec((c, HEAD_DIM), lambda bi, h, i: (bi * nc + i, h)),
                   pl.BlockSpec((1, 1, HEAD_DIM, HEAD_DIM), lambda bi, h, i: (bi, h, 0, 0))],
        out_shape=[jax.ShapeDtypeStruct((b * t, GROUP_WIDTH), BF16),
                   jax.ShapeDtypeStruct((b, N_HEADS, HEAD_DIM, HEAD_DIM), F32)],
        scratch_shapes=[pltpu.VMEM((HEAD_DIM, HEAD_DIM), F32)],
        compiler_params=_cparams(("parallel", "parallel", "arbitrary")),
        name="hgrn_prompt",
    )(proj, proj, proj, proj, lb_logits, g_hg.reshape(1, GROUP_WIDTH))


def _conv_prompt_kernel(u_ref, b_ref, c_ref, g_ref, w_ref, o_ref, st_ref, carry_ref, *, tt):
    ti = pl.program_id(1)
    nt = pl.num_programs(1)

    @pl.when(ti == 0)
    def _():
        carry_ref[...] = jnp.zeros_like(carry_ref)

    z = c_ref[0] * u_ref[0]
    row = lax.broadcasted_iota(jnp.int32, z.shape, 0)
    p0 = carry_ref[0:1, :]
    p1 = carry_ref[1:2, :]
    z1 = jnp.where(row == 0, p1, pltpu.roll(z, 1, 0))
    z2 = jnp.where(row == 0, p0, jnp.where(row == 1, p1, pltpu.roll(z, 2, 0)))
    y = z2 * w_ref[0:1, :] + z1 * w_ref[1:2, :] + z * w_ref[2:3, :]
    o_ref[...] = (b_ref[0] * y * _silu(g_ref[0])).astype(o_ref.dtype)
    tail = z[tt - 2:tt, :]
    carry_ref[0:2, :] = tail

    @pl.when(ti == nt - 1)
    def _():
        st_ref[0] = tail


def _conv_prompt(proj, conv_w_l, b, t, tt=512):
    tt = min(tt, t)
    nt = t // tt
    spec = lambda part: pl.BlockSpec((1, tt, GROUP_WIDTH), lambda bi, i: (part, bi * nt + i, 0))
    return pl.pallas_call(
        functools.partial(_conv_prompt_kernel, tt=tt),
        grid=(b, nt),
        in_specs=[spec(P_CV_U), spec(P_CV_B), spec(P_CV_C), spec(P_CV_G),
                  pl.BlockSpec((CONV_WIDTH, GROUP_WIDTH), lambda bi, i: (0, 0))],
        out_specs=[pl.BlockSpec((tt, GROUP_WIDTH), lambda bi, i: (bi * nt + i, 0)),
                   pl.BlockSpec((1, CONV_WIDTH - 1, GROUP_WIDTH), lambda bi, i: (bi, 0, 0))],
        out_shape=[jax.ShapeDtypeStruct((b * t, GROUP_WIDTH), BF16),
                   jax.ShapeDtypeStruct((b, CONV_WIDTH - 1, GROUP_WIDTH), F32)],
        scratch_shapes=[pltpu.VMEM((8, GROUP_WIDTH), F32)],
        compiler_params=_cparams(("parallel", "arbitrary")),
        name="conv_prompt",
    )(proj, proj, proj, proj, conv_w_l)


def _lane_pick(mat, idx):
    lane = lax.broadcasted_iota(jnp.int32, mat.shape, 1)
    return jnp.max(jnp.where(lane == idx, mat, NEG_INF), axis=-1, keepdims=True)


def _moba_select_t(gate_t, n_past, n_blocks):
    row = lax.broadcasted_iota(jnp.int32, gate_t.shape, 0)
    gm = jnp.where(row < n_past, gate_t, NEG_INF)
    sel = jnp.full(gate_t.shape, NEG_INF, F32)
    for n in range(n_blocks):
        cur = gm[n:n + 1, :]
        beats = jnp.where(gm > cur, 1.0, jnp.where(jnp.logical_and(gm == cur, row < n), 1.0, 0.0))
        rank = jnp.sum(beats, axis=0, keepdims=True)
        ok = jnp.logical_and(rank < MOBA_TOPK, cur > NEG_INF)
        sel = jnp.where(jnp.logical_and(row == n, ok), 0.0, sel)
    return sel


def _mb_prompt_kernel(q_ref, k_ref, v_ref, g_ref, o_ref, kmean_ref, kb_ref, vt_ref, *, nb):
    qi = pl.program_id(2)
    blk = MOBA_BLOCK

    @pl.when(qi == 0)
    def _():
        kmean_ref[...] = jnp.zeros_like(kmean_ref)
        for n in range(nb):
            kn = k_ref[0, n * blk:(n + 1) * blk, :]
            kmean_ref[n:n + 1, :] = jnp.mean(kn, axis=0, keepdims=True)
            kb_ref[n * blk:(n + 1) * blk, :] = kn.astype(BF16)
            vt_ref[:, n * blk:(n + 1) * blk] = v_ref[0, n * blk:(n + 1) * blk, :].T.astype(BF16)

    q = q_ref[0]
    qb = q.astype(BF16)
    gate_t = lax.dot_general(kmean_ref[...], q, (((1,), (1,)), ((), ())),
                             precision=lax.Precision.HIGHEST, preferred_element_type=F32)
    sel_t = _moba_select_t(gate_t, qi, nb)

    rel = (lax.broadcasted_iota(jnp.int32, (blk, blk), 0)
           - lax.broadcasted_iota(jnp.int32, (blk, blk), 1))
    s_all = _dot_nt(kb_ref[...], qb)
    parts = []
    for j in range(nb):
        s = s_all[j * blk:(j + 1) * blk] * SCALE + jnp.where(j < qi, sel_t[j:j + 1, :], 0.0)
        parts.append(jnp.where(rel <= (qi - j) * blk, s, NEG_INF))
    m = functools.reduce(jnp.maximum, [jnp.max(s, axis=0, keepdims=True) for s in parts])
    ps = [jnp.exp(s - m) for s in parts]
    l = functools.reduce(lambda a, c: a + c, [jnp.sum(p, axis=0, keepdims=True) for p in ps])
    p_all = jnp.concatenate([p.astype(BF16) for p in ps], axis=0)
    acc = _dot(vt_ref[...], p_all)
    o_ref[...] = ((acc / l).T * _silu(g_ref[0])).astype(o_ref.dtype)


def _mb_prompt(proj, b, t):
    blk = MOBA_BLOCK
    nb = t // blk
    nb_pad = -(-nb // 8) * 8
    qspec = lambda part: pl.BlockSpec((1, blk, HEAD_DIM), lambda bi, h, i: (part, bi * nb + i, h))
    kvspec = lambda part: pl.BlockSpec((1, t, HEAD_DIM), lambda bi, h, i: (part, bi, h))
    return pl.pallas_call(
        functools.partial(_mb_prompt_kernel, nb=nb),
        grid=(b, N_HEADS, nb),
        in_specs=[qspec(P_MB_Q), kvspec(P_MB_K), kvspec(P_MB_V), qspec(P_MB_G)],
        out_specs=pl.BlockSpec((blk, HEAD_DIM), lambda bi, h, i: (bi * nb + i, h)),
        out_shape=jax.ShapeDtypeStruct((b * t, GROUP_WIDTH), BF16),
        scratch_shapes=[pltpu.VMEM((nb_pad, HEAD_DIM), F32), pltpu.VMEM((t, HEAD_DIM), BF16),
                        pltpu.VMEM((HEAD_DIM, t), BF16)],
        compiler_params=_cparams(("parallel", "parallel", "arbitrary")),
        name="moba_prompt",
    )(proj, proj, proj, proj)


def _stack_heads(x):
    return jnp.concatenate([x[:, h * HEAD_DIM:(h + 1) * HEAD_DIM] for h in range(N_HEADS)], axis=0)


def _spread_row(x, j, t):
    return jnp.concatenate(
        [jnp.broadcast_to(x[j:j + 1, h * HEAD_DIM:(h + 1) * HEAD_DIM], (t, HEAD_DIM))
         for h in range(N_HEADS)], axis=0)


def _store_heads(o_ref, acc, gate, t):
    for h in range(N_HEADS):
        sl = slice(h * HEAD_DIM, (h + 1) * HEAD_DIM)
        o_ref[0, :, sl] = acc[h * t:(h + 1) * t, :] * _silu(gate[:, sl])


def _own_head(rows, cols, t):
    r = lax.broadcasted_iota(jnp.int32, (rows, cols), 0)
    c = lax.broadcasted_iota(jnp.int32, (rows, cols), 1)
    return c % N_HEADS == r // t


def _sb_decode_kernel(pt_ref, q_ref, kn_ref, vn_ref, g_ref, *rest, t, n_group):
    k_refs = rest[:n_group]
    v_refs = rest[n_group:2 * n_group]
    o_ref, acc_ref, run_ref, qr_ref = rest[2 * n_group:]
    s = pl.program_id(1)
    ns = pl.num_programs(1)
    pr = k_refs[0].shape[0]
    rows = N_HEADS * t

    @pl.when(s == 0)
    def _():
        qr = _stack_heads(q_ref[0, 0])
        qr_ref[...] = qr
        tq = lax.broadcasted_iota(jnp.int32, (rows, 1), 0) % t
        kn = kn_ref[0, 0]
        vn = vn_ref[0, 0]
        acc = jnp.zeros((rows, HEAD_DIM), F32)
        run = jnp.zeros((rows, 1), F32)
        for j in range(t - 1, -1, -1):
            z = jnp.sum(qr * _spread_row(kn, j, t), axis=-1, keepdims=True) * SCALE
            ok = tq > j
            lk = jnp.where(ok, -_softplus(z), 0.0)
            w = jnp.where(ok, jnp.exp(z + lk + run), 0.0)
            acc = acc + w * _spread_row(vn, j, t)
            run = run + lk
        acc_ref[...] = acc
        run_ref[...] = jnp.broadcast_to(run, run_ref.shape)

    qb = qr_ref[...].astype(BF16)
    tri = _strict_upper_ones(HEAD_DIM)
    own = _own_head(rows, n_group * pr, t)
    z = jnp.concatenate([_dot_nt(qb, k_refs[gi][...].astype(BF16)) for gi in range(n_group)],
                        axis=1) * SCALE
    lk = jnp.where(own, -_softplus(z), 0.0)
    between, run = _reverse_cumsum_groups(lk, run_ref[:, 0:1], tri)
    w = jnp.where(own, jnp.exp(z + lk + between), 0.0).astype(BF16)
    acc = acc_ref[...]
    for gi in range(n_group):
        acc = acc + _dot(w[:, gi * pr:(gi + 1) * pr], v_refs[gi][...].astype(BF16))
    acc_ref[...] = acc
    run_ref[...] = jnp.broadcast_to(run, run_ref.shape)

    @pl.when(s == ns - 1)
    def _():
        _store_heads(o_ref, acc, g_ref[0, 0], t)


def _page_specs(layer, n_steps, n_group, page_rows, reverse):
    specs = []
    for gi in range(n_group):
        if reverse:
            imap = lambda bi, s, pt, gi=gi: (layer, pt[bi, (n_steps - 1 - s) * n_group + gi], 0, 0)
        else:
            imap = lambda bi, s, pt, gi=gi: (layer, pt[bi, s * n_group + gi], 0, 0)
        specs.append(pl.BlockSpec((None, None, page_rows, HEAD_DIM), imap))
    return specs


def _sb_decode(proj4, cache_k, cache_v, page_table, layer, n_group=8):
    _, bd, t, _ = proj4.shape
    n_pages = page_table.shape[1]
    pr = cache_k.shape[2]
    n_group = min(n_group, n_pages)
    n_steps = n_pages // n_group
    tok = lambda part: pl.BlockSpec((1, 1, t, GROUP_WIDTH), lambda bi, s, pt: (part, bi, 0, 0))
    rows = N_HEADS * t
    grid_spec = pltpu.PrefetchScalarGridSpec(
        num_scalar_prefetch=1,
        grid=(bd, n_steps),
        in_specs=[tok(P_SB_Q), tok(P_SB_K), tok(P_SB_V), tok(P_SB_G)]
        + _page_specs(layer, n_steps, n_group, pr, True)
        + _page_specs(layer, n_steps, n_group, pr, True),
        out_specs=pl.BlockSpec((1, t, GROUP_WIDTH), lambda bi, s, pt: (bi, 0, 0)),
        scratch_shapes=[pltpu.VMEM((rows, HEAD_DIM), F32),
                        pltpu.VMEM((rows, HEAD_DIM), F32),
                        pltpu.VMEM((rows, HEAD_DIM), F32)],
    )
    return pl.pallas_call(
        functools.partial(_sb_decode_kernel, t=t, n_group=n_group),
        grid_spec=grid_spec,
        out_shape=jax.ShapeDtypeStruct((bd, t, GROUP_WIDTH), F32),
        compiler_params=_cparams(("parallel", "arbitrary")),
        name="sb_decode",
    )(page_table, proj4, proj4, proj4, proj4, *([cache_k] * n_group), *([cache_v] * n_group))


def _mb_gate_kernel(pt_ref, q_ref, *rest, t, n_group):
    k_refs = rest[:n_group]
    gate_ref, logit_ref, qr_ref = rest[n_group:]
    s = pl.program_id(1)
    pr = k_refs[0].shape[0]
    per_blk = MOBA_BLOCK * N_HEADS // pr

    @pl.when(s == 0)
    def _():
        qr_ref[...] = _stack_heads(q_ref[0, 0])
        gate_ref[...] = jnp.full(gate_ref.shape, NEG_INF, F32)

    qr = qr_ref[...]
    qb = qr.astype(BF16)
    lane = lax.broadcasted_iota(jnp.int32, gate_ref.shape[1:], 1)
    gates = gate_ref[0]
    for bi in range(n_group // per_blk):
        ksum = jnp.zeros((8, HEAD_DIM), F32)
        for pi in range(per_blk):
            gi = bi * per_blk + pi
            k = k_refs[gi][...]
            ksum = ksum + jnp.sum(k.reshape(pr // 8, 8, HEAD_DIM), axis=0)
            logit_ref[0, :, gi * pr:(gi + 1) * pr] = _dot_nt(qb, k.astype(BF16))
        kmean = (ksum[0:N_HEADS] + ksum[N_HEADS:2 * N_HEADS]) * (1.0 / MOBA_BLOCK)
        kmean_rows = jnp.concatenate(
            [jnp.broadcast_to(kmean[h:h + 1], (t, HEAD_DIM)) for h in range(N_HEADS)], axis=0)
        col = jnp.sum(qr * kmean_rows, axis=-1, keepdims=True)
        gates = jnp.where(lane == s * (n_group // per_blk) + bi, col, gates)
    gate_ref[0] = gates


def _mb_attend_kernel(pt_ref, q_ref, kn_ref, vn_ref, g_ref, gate_ref, logit_ref, *rest,
                      t, n_group, n_past):
    v_refs = rest[:n_group]
    o_ref, acc_ref, m_ref, l_ref, sel_ref = rest[n_group:]
    s = pl.program_id(1)
    ns = pl.num_programs(1)
    pr = v_refs[0].shape[0]
    per_blk = MOBA_BLOCK * N_HEADS // pr
    rows = N_HEADS * t

    @pl.when(s == 0)
    def _():
        neg = lambda n: jnp.full((n, HEAD_DIM), NEG_INF, F32)
        nbp = -(-n_past // 8) * 8
        gate_t = jnp.concatenate([gate_ref[0], neg(HEAD_DIM - rows)], axis=0).T
        sel_t = _moba_select_t(gate_t[:nbp], n_past, n_past)
        sel_ref[...] = jnp.concatenate([sel_t, neg(HEAD_DIM - nbp)], axis=0).T[:rows]
        qr = _stack_heads(q_ref[0, 0])
        tq = lax.broadcasted_iota(jnp.int32, (rows, 1), 0) % t
        kn = kn_ref[0, 0]
        vn = vn_ref[0, 0]
        zs = []
        for j in range(t):
            z = jnp.sum(qr * _spread_row(kn, j, t), axis=-1, keepdims=True) * SCALE
            zs.append(jnp.where(tq >= j, z, NEG_INF))
        m = functools.reduce(jnp.maximum, zs)
        l = jnp.zeros((rows, 1), F32)
        acc = jnp.zeros((rows, HEAD_DIM), F32)
        for j in range(t):
            p = jnp.exp(zs[j] - m)
            l = l + p
            acc = acc + p * _spread_row(vn, j, t)
        acc_ref[...] = acc
        m_ref[...] = jnp.broadcast_to(m, m_ref.shape)
        l_ref[...] = jnp.broadcast_to(l, l_ref.shape)

    acc = acc_ref[...]
    m = m_ref[:, 0:1]
    l = l_ref[:, 0:1]
    sel = sel_ref[...]
    own = _own_head(rows, n_group * pr, t)
    n_blk = n_group // per_blk
    bias = jnp.concatenate(
        [jnp.broadcast_to(_lane_pick(sel, s * n_blk + bi), (rows, per_blk * pr)) for bi in range(n_blk)],
        axis=1)
    sc = jnp.where(own, logit_ref[0] * SCALE + bias, NEG_INF)
    m_new = jnp.maximum(m, jnp.max(sc, axis=-1, keepdims=True))
    alpha = jnp.exp(m - m_new)
    p = jnp.exp(sc - m_new)
    l = alpha * l + jnp.sum(p, axis=-1, keepdims=True)
    pb = p.astype(BF16)
    acc = alpha * acc
    for gi in range(n_group):
        acc = acc + _dot(pb[:, gi * pr:(gi + 1) * pr], v_refs[gi][...].astype(BF16))
    m = m_new
    acc_ref[...] = acc
    m_ref[...] = jnp.broadcast_to(m, m_ref.shape)
    l_ref[...] = jnp.broadcast_to(l, l_ref.shape)

    @pl.when(s == ns - 1)
    def _():
        _store_heads(o_ref, acc / l, g_ref[0, 0], t)


def _mb_decode(proj4, cache_k, cache_v, page_table, layer, n_group=8):
    _, bd, t, _ = proj4.shape
    n_pages = page_table.shape[1]
    ps = cache_k.shape[2]
    n_group = min(n_group, n_pages)
    n_steps = n_pages // n_group
    n_past = n_pages * ps // (MOBA_BLOCK * N_HEADS)
    rows = N_HEADS * t
    tok = lambda part: pl.BlockSpec((1, 1, t, GROUP_WIDTH), lambda bi, s, pt: (part, bi, 0, 0))
    gate_spec = pl.BlockSpec((1, rows, HEAD_DIM), lambda bi, s, pt: (bi, 0, 0))
    logit_spec = pl.BlockSpec((1, rows, n_group * ps), lambda bi, s, pt: (bi, 0, s))

    gates, logits = pl.pallas_call(
        functools.partial(_mb_gate_kernel, t=t, n_group=n_group),
        grid_spec=pltpu.PrefetchScalarGridSpec(
            num_scalar_prefetch=1,
            grid=(bd, n_steps),
            in_specs=[tok(P_MB_Q)] + _page_specs(layer, n_steps, n_group, ps, False),
            out_specs=[gate_spec, logit_spec],
            scratch_shapes=[pltpu.VMEM((rows, HEAD_DIM), F32)],
        ),
        out_shape=[jax.ShapeDtypeStruct((bd, rows, HEAD_DIM), F32),
                   jax.ShapeDtypeStruct((bd, rows, n_pages * ps), F32)],
        compiler_params=_cparams(("parallel", "arbitrary")),
        name="moba_decode_gate",
    )(page_table, proj4, *([cache_k] * n_group))

    return pl.pallas_call(
        functools.partial(_mb_attend_kernel, t=t, n_group=n_group, n_past=n_past),
        grid_spec=pltpu.PrefetchScalarGridSpec(
            num_scalar_prefetch=1,
            grid=(bd, n_steps),
            in_specs=[tok(P_MB_Q), tok(P_MB_K), tok(P_MB_V), tok(P_MB_G), gate_spec, logit_spec]
            + _page_specs(layer, n_steps, n_group, ps, False),
            out_specs=pl.BlockSpec((1, t, GROUP_WIDTH), lambda bi, s, pt: (bi, 0, 0)),
            scratch_shapes=[pltpu.VMEM((rows, HEAD_DIM), F32),
                            pltpu.VMEM((rows, HEAD_DIM), F32),
                            pltpu.VMEM((rows, HEAD_DIM), F32),
                            pltpu.VMEM((rows, HEAD_DIM), F32)],
        ),
        out_shape=jax.ShapeDtypeStruct((bd, t, GROUP_WIDTH), F32),
        compiler_params=_cparams(("parallel", "arbitrary")),
        name="moba_decode_attend",
    )(page_table, proj4, proj4, proj4, proj4, gates, logits, *([cache_v] * n_group))


def _small_decode_kernel(hq_ref, hf_ref, hi_ref, hgate_ref, cu_ref, cb_ref, cc_ref, cg_ref,
                         s0_ref, cv0_ref, lb_ref, gn_ref, w_ref,
                         ohg_ref, s_out_ref, ocv_ref, cv_out_ref, *, t, layer):
    d = HEAD_DIM
    log_lb, log_1m_lb = _lower_bound_logs(lb_ref[...], layer)
    g_all = _log_forget(hf_ref[0, 0], log_lb, log_1m_lb)
    k_all = 1.0 - jnp.exp(g_all)
    q_all = hq_ref[0, 0] * SCALE
    v_all = hi_ref[0, 0]
    gate_all = hgate_ref[0, 0]
    gn = gn_ref[...]
    pad = jnp.zeros((8 - t % 8, d), F32) if t % 8 else None

    for h in range(N_HEADS):
        sl = slice(h * d, (h + 1) * d)
        g, kk, qs, v = g_all[:, sl], k_all[:, sl], q_all[:, sl], v_all[:, sl]
        s0 = s0_ref[0, h]
        gcs = []
        run = jnp.zeros((1, d), F32)
        for i in range(t):
            run = run + g[i:i + 1]
            gcs.append(run)
        qe = jnp.concatenate([qs[i:i + 1] * jnp.exp(gcs[i]) for i in range(t)], axis=0)
        if pad is not None:
            qe = jnp.concatenate([qe, pad], axis=0)
        inter = _dot(qe.astype(BF16), s0.astype(BF16))
        s_new = _col_bcast(jnp.exp(gcs[t - 1])) * s0
        for j in range(t):
            s_new = s_new + _col_bcast(kk[j:j + 1] * jnp.exp(gcs[t - 1] - gcs[j])) * v[j:j + 1]
        s_out_ref[0, h] = s_new
        for i in range(t):
            o = inter[i:i + 1]
            for j in range(i + 1):
                sc = jnp.sum(qs[i:i + 1] * kk[j:j + 1] * jnp.exp(gcs[i] - gcs[j]),
                             axis=-1, keepdims=True)
                o = o + sc * v[j:j + 1]
            o = _head_rmsnorm(o, gn[:, sl]) * _silu(gate_all[i:i + 1, sl])
            ohg_ref[0, i:i + 1, sl] = o

    z = cc_ref[0, 0] * cu_ref[0, 0]
    zz = [cv0_ref[0, j:j + 1, :] for j in range(CONV_WIDTH - 1)] + [z[i:i + 1] for i in range(t)]
    cb = cb_ref[0, 0]
    cg = cg_ref[0, 0]
    for i in range(t):
        y = zz[i] * w_ref[0:1, :]
        for j in range(1, CONV_WIDTH):
            y = y + zz[i + j] * w_ref[j:j + 1, :]
        ocv_ref[0, i:i + 1, :] = cb[i:i + 1] * y * _silu(cg[i:i + 1])
    for j in range(CONV_WIDTH - 1):
        cv_out_ref[0, j:j + 1, :] = zz[t + j]


def _small_decode(proj4, state_hgrn, state_conv, lb_logits, g_hg, conv_w, layer):
    _, bd, t, _ = proj4.shape
    depth = lb_logits.shape[0]
    d = HEAD_DIM
    tok = lambda part: pl.BlockSpec((1, 1, t, GROUP_WIDTH), lambda bi: (part, bi, 0, 0))
    return pl.pallas_call(
        functools.partial(_small_decode_kernel, t=t, layer=layer),
        grid=(bd,),
        in_specs=[tok(P_HG_Q), tok(P_HG_F), tok(P_HG_I), tok(P_HG_G),
                  tok(P_CV_U), tok(P_CV_B), tok(P_CV_C), tok(P_CV_G),
                  pl.BlockSpec((None, 1, N_HEADS, d, d), lambda bi: (layer, bi, 0, 0, 0)),
                  pl.BlockSpec((None, 1, CONV_WIDTH - 1, GROUP_WIDTH), lambda bi: (layer, bi, 0, 0)),
                  pl.BlockSpec((depth, GROUP_WIDTH), lambda bi: (0, 0)),
                  pl.BlockSpec((1, GROUP_WIDTH), lambda bi: (0, 0)),
                  pl.BlockSpec((None, CONV_WIDTH, GROUP_WIDTH), lambda bi: (layer, 0, 0))],
        out_specs=[pl.BlockSpec((1, t, GROUP_WIDTH), lambda bi: (bi, 0, 0)),
                   pl.BlockSpec((1, N_HEADS, d, d), lambda bi: (bi, 0, 0, 0)),
                   pl.BlockSpec((1, t, GROUP_WIDTH), lambda bi: (bi, 0, 0)),
                   pl.BlockSpec((1, CONV_WIDTH - 1, GROUP_WIDTH), lambda bi: (bi, 0, 0))],
        out_shape=[jax.ShapeDtypeStruct((bd, t, GROUP_WIDTH), F32),
                   jax.ShapeDtypeStruct((bd, N_HEADS, d, d), F32),
                   jax.ShapeDtypeStruct((bd, t, GROUP_WIDTH), F32),
                   jax.ShapeDtypeStruct((bd, CONV_WIDTH - 1, GROUP_WIDTH), F32)],
        compiler_params=_cparams(("parallel",)),
        name="hgrn_conv_decode",
    )(*([proj4] * 8), state_hgrn, state_conv, lb_logits, g_hg.reshape(1, GROUP_WIDTH), conv_w)


def _row_tile(m, pref):
    return pref if m % pref == 0 else m


def kernel(x_prompt, x_sample, cache_sb_k, cache_sb_v, cache_moba_k, cache_moba_v, state_hgrn, state_conv,
           page_table, w_in, w_out, norm_pre, norm_post, hgrn_out_norm, conv_w, hgrn_lb_logits):
    b, t, d = x_prompt.shape
    bd, td, _ = x_sample.shape
    depth = w_in.shape[0]
    n_phys, ps = cache_sb_k.shape[1], cache_sb_k.shape[2]
    past_len = page_table.shape[1] * ps
    assert past_len % MOBA_BLOCK == 0 and td <= MOBA_BLOCK and t % MOBA_BLOCK == 0

    mp, ms = b * t, bd * td
    tm_p, tm_s = _row_tile(mp, 512), _row_tile(ms, 512)
    tab_p = _rope_tables(jnp.arange(t, dtype=jnp.int32))
    pos_s = past_len + jnp.arange(td, dtype=jnp.int32)
    tab_s = tuple(jnp.tile(a, (bd, 1)) for a in _rope_tables(pos_s))
    assert t % tm_p == 0 or tm_p % t == 0
    if tm_p > t:
        tab_p = tuple(jnp.tile(a, (tm_p // t, 1)) for a in tab_p)

    w_in_b = w_in.astype(BF16)
    w_out_b = w_out.astype(BF16).reshape(depth, 4, GROUP_WIDTH, d)
    pool = lambda c: c.reshape(depth, n_phys, ps * N_HEADS, HEAD_DIM)
    c_sb_k, c_sb_v, c_mb_k, c_mb_v = pool(cache_sb_k), pool(cache_sb_v), pool(cache_moba_k), pool(cache_moba_v)

    xp = x_prompt.reshape(mp, d)
    xs = x_sample.reshape(ms, d)
    outs_p = [[] for _ in range(6)]
    outs_s = [[] for _ in range(6)]
    heads = lambda a, n, tt: a.reshape(n, tt, N_HEADS, HEAD_DIM)
    for l in range(depth):
        proj = _project(xp, norm_pre[l], w_in_b[l], tab_p, tm_p)
        o_sb = _sb_prompt(proj, b, t)
        o_hg, hg_new = _hg_prompt(proj, hgrn_lb_logits, hgrn_out_norm[l], l, b, t)
        o_cv, cv_new = _conv_prompt(proj, conv_w[l], b, t)
        o_mb = _mb_prompt(proj, b, t)
        xp = _out_project((o_sb, o_hg, o_cv, o_mb), w_out_b[l], xp, norm_post[l], tm_p)
        for lst, a in zip(outs_p, (heads(proj[P_SB_K], b, t), heads(proj[P_SB_V], b, t),
                                   heads(proj[P_MB_K], b, t), heads(proj[P_MB_V], b, t), hg_new, cv_new)):
            lst.append(a)

        proj_s = _project(xs, norm_pre[l], w_in_b[l], tab_s, tm_s)
        proj4 = proj_s.reshape(N_PARTS, bd, td, GROUP_WIDTH)
        s_sb = _sb_decode(proj4, c_sb_k, c_sb_v, page_table, l)
        s_hg, hg_new_s, s_cv, cv_new_s = _small_decode(proj4, state_hgrn, state_conv, hgrn_lb_logits,
                                                       hgrn_out_norm[l], conv_w, l)
        s_mb = _mb_decode(proj4, c_mb_k, c_mb_v, page_table, l)
        mixed_s = tuple(a.reshape(ms, GROUP_WIDTH).astype(BF16) for a in (s_sb, s_hg, s_cv, s_mb))
        xs = _out_project(mixed_s, w_out_b[l], xs, norm_post[l], tm_s)
        for lst, a in zip(outs_s, (heads(proj_s[P_SB_K], bd, td), heads(proj_s[P_SB_V], bd, td),
                                   heads(proj_s[P_MB_K], bd, td), heads(proj_s[P_MB_V], bd, td),
                                   hg_new_s, cv_new_s)):
            lst.append(a)

    return (xp.reshape(b, t, d), xs.reshape(bd, td, d),
            *(jnp.stack(o) for o in outs_p), *(jnp.stack(o) for o in outs_s))
```

```python
import functools

import jax
import jax.numpy as jnp
from jax import lax
from jax.experimental import pallas as pl
from jax.experimental.pallas import tpu as pltpu

F32 = jnp.float32
BF16 = jnp.bfloat16

HEAD_DIM = 128
N_HEADS = 4
GROUP_WIDTH = N_HEADS * HEAD_DIM
N_PARTS = 16
CONV_WIDTH = 3
MOBA_BLOCK = 256
MOBA_TOPK = 3
ROPE_THETA = 500000.0
ROPE_HALF = HEAD_DIM // 8
RMS_EPS = 1e-6
SCALE = HEAD_DIM ** -0.5
NEG_INF = float("-inf")

(P_SB_Q, P_SB_K, P_SB_V, P_SB_G, P_HG_Q, P_HG_F, P_HG_I, P_HG_G,
 P_CV_U, P_CV_B, P_CV_C, P_CV_G, P_MB_Q, P_MB_K, P_MB_V, P_MB_G) = range(N_PARTS)

STEP_PARTS = (P_SB_K, P_SB_V, P_MB_K, P_MB_V,
              P_HG_F, P_HG_Q, P_HG_I, P_MB_Q,
              P_SB_Q, P_SB_G, P_HG_G, P_CV_U, P_CV_B, P_CV_C, P_CV_G, P_MB_G)
N_KV, N_MISC, N_LO = 4, 4, 8
KV_SB_K, KV_SB_V, KV_MB_K, KV_MB_V = range(N_KV)
MISC_HG_F, MISC_HG_Q, MISC_HG_I, MISC_MB_Q = range(N_MISC)
LO_SB_Q, LO_SB_G, LO_HG_G, LO_CV_U, LO_CV_B, LO_CV_C, LO_CV_G, LO_MB_G = range(N_LO)
STEP_MB_K = STEP_PARTS.index(P_MB_K)
STEP_MB_Q = STEP_PARTS.index(P_MB_Q)

VMEM_LIMIT = 48 * 1024 * 1024


def _cparams(sem):
    return pltpu.CompilerParams(dimension_semantics=sem, vmem_limit_bytes=VMEM_LIMIT)


def _softplus(z):
    return jnp.maximum(z, 0.0) + jnp.log(1.0 + jnp.exp(-jnp.abs(z)))


def _silu(g):
    return g / (1.0 + jnp.exp(-g))


def _split2(x):
    hi = x.astype(BF16)
    lo = (x - hi.astype(F32)).astype(BF16)
    return hi, lo


def _split3(x):
    hi = x.astype(BF16)
    r = x - hi.astype(F32)
    mid = r.astype(BF16)
    lo = (r - mid.astype(F32)).astype(BF16)
    return hi, mid, lo


def _dot(a, b):
    return jnp.dot(a, b, preferred_element_type=F32)


def _dot_nt(a, b):
    return lax.dot_general(a, b, (((1,), (1,)), ((), ())), preferred_element_type=F32)


def _dot_tn(a, b):
    return lax.dot_general(a, b, (((0,), (0,)), ((), ())), preferred_element_type=F32)


def _head_rows(start, n):
    return pl.ds(start, n, stride=N_HEADS)


def _prenorm_kernel(x_ref, g_ref, h_ref):
    x = x_ref[...]
    ms = jnp.mean(x * x, axis=-1, keepdims=True)
    h_ref[...] = (x * lax.rsqrt(ms + RMS_EPS) * g_ref[...]).astype(h_ref.dtype)


def _prenorm(x, g_pre, tm):
    m, d = x.shape
    return pl.pallas_call(
        _prenorm_kernel,
        grid=(m // tm,),
        in_specs=[pl.BlockSpec((tm, d), lambda i: (i, 0)), pl.BlockSpec((1, d), lambda i: (0, 0))],
        out_specs=pl.BlockSpec((tm, d), lambda i: (i, 0)),
        out_shape=jax.ShapeDtypeStruct((m, d), BF16),
        compiler_params=_cparams(("parallel",)),
        name="prenorm",
    )(x, g_pre.reshape(1, d))


def _proj_kernel(h_ref, w_ref, c_ref, s1_ref, s2_ref, *rest, tm, n_alias):
    kv_refs = rest[n_alias:n_alias + N_KV]
    misc_ref, lo_ref = rest[n_alias + N_KV:]
    n = pl.program_id(1)
    acc = _dot(h_ref[...], w_ref[...])

    def head(hh, rope):
        a = acc[:, hh * HEAD_DIM:(hh + 1) * HEAD_DIM]
        if rope:
            a = (a * c_ref[...]
                 + pltpu.roll(a, HEAD_DIM - ROPE_HALF, 1) * s1_ref[...]
                 + pltpu.roll(a, ROPE_HALF, 1) * s2_ref[...])
        return a

    for a in range(N_KV):
        @pl.when(n == a)
        def _(a=a):
            for hh in range(N_HEADS):
                kv_refs[a][0, _head_rows(hh, tm), :] = head(hh, a == STEP_MB_K)

    @pl.when(jnp.logical_and(jnp.logical_and(n >= N_KV, n < N_KV + N_MISC), n != STEP_MB_Q))
    def _():
        misc_ref[0] = acc

    @pl.when(n == STEP_MB_Q)
    def _():
        for hh in range(N_HEADS):
            misc_ref[0, :, hh * HEAD_DIM:(hh + 1) * HEAD_DIM] = head(hh, True)

    @pl.when(n >= N_KV + N_MISC)
    def _():
        lo_ref[0] = acc.astype(lo_ref.dtype)


def _rope_tables(pos):
    inv_freq = ROPE_THETA ** (-jnp.arange(ROPE_HALF, dtype=F32) / ROPE_HALF)
    ang = pos.astype(F32)[:, None] * inv_freq[None, :]
    cos, sin = jnp.cos(ang), jnp.sin(ang)
    n = pos.shape[0]
    rest = HEAD_DIM - 2 * ROPE_HALF
    c = jnp.concatenate([cos, cos, jnp.ones((n, rest), F32)], axis=1)
    s1 = jnp.concatenate([-sin, jnp.zeros((n, HEAD_DIM - ROPE_HALF), F32)], axis=1)
    s2 = jnp.concatenate([jnp.zeros((n, ROPE_HALF), F32), sin, jnp.zeros((n, rest), F32)], axis=1)
    return c, s1, s2


def _project(h, w_steps, tables, tm, layer, depth, kv_prev, lo_dtype):
    m, d = h.shape
    period_blocks = tables[0].shape[0] // tm
    tab_spec = pl.BlockSpec((tm, HEAD_DIM), lambda i, n: (i % period_blocks, 0))
    n_alias = len(kv_prev)
    clamp = lambda v, hi: jnp.minimum(jnp.maximum(v, 0), hi)
    kv_spec = pl.BlockSpec((1, tm * N_HEADS, HEAD_DIM), lambda i, n: (layer, i, 0))
    outs = pl.pallas_call(
        functools.partial(_proj_kernel, tm=tm, n_alias=n_alias),
        grid=(m // tm, N_PARTS),
        in_specs=[
            pl.BlockSpec((tm, d), lambda i, n: (i, 0)),
            pl.BlockSpec((d, GROUP_WIDTH), lambda i, n: (0, n)),
            tab_spec, tab_spec, tab_spec,
        ] + [pl.BlockSpec(memory_space=pl.ANY)] * n_alias,
        out_specs=[kv_spec] * N_KV + [
            pl.BlockSpec((1, tm, GROUP_WIDTH), lambda i, n: (clamp(n - N_KV, N_MISC - 1), i, 0)),
            pl.BlockSpec((1, tm, GROUP_WIDTH), lambda i, n: (clamp(n - N_KV - N_MISC, N_LO - 1), i, 0)),
        ],
        out_shape=[jax.ShapeDtypeStruct((depth, m * N_HEADS, HEAD_DIM), F32)] * N_KV + [
            jax.ShapeDtypeStruct((N_MISC, m, GROUP_WIDTH), F32),
            jax.ShapeDtypeStruct((N_LO, m, GROUP_WIDTH), lo_dtype),
        ],
        input_output_aliases={5 + a: a for a in range(n_alias)},
        compiler_params=_cparams(("parallel", "arbitrary")),
        name="in_proj",
    )(h, w_steps, *tables, *kv_prev)
    return list(outs[:N_KV]), outs[N_KV], outs[N_KV + 1]


def _out_kernel(a_ref, b_ref, c_ref, d_ref, w_ref, x_ref, g_ref, o_ref):
    acc = _dot(a_ref[...], w_ref[0])
    acc += _dot(b_ref[...], w_ref[1])
    acc += _dot(c_ref[...], w_ref[2])
    acc += _dot(d_ref[...], w_ref[3])
    ms = jnp.mean(acc * acc, axis=-1, keepdims=True)
    o_ref[...] = x_ref[...] + acc * lax.rsqrt(ms + RMS_EPS) * g_ref[...]


def _out_project(mixed, w_out, x, g_post, tm):
    m, d = x.shape
    mix_spec = pl.BlockSpec((tm, GROUP_WIDTH), lambda i: (i, 0))
    return pl.pallas_call(
        _out_kernel,
        grid=(m // tm,),
        in_specs=[mix_spec, mix_spec, mix_spec, mix_spec,
                  pl.BlockSpec((4, GROUP_WIDTH, d), lambda i: (0, 0, 0)),
                  pl.BlockSpec((tm, d), lambda i: (i, 0)),
                  pl.BlockSpec((1, d), lambda i: (0, 0))],
        out_specs=pl.BlockSpec((tm, d), lambda i: (i, 0)),
        out_shape=jax.ShapeDtypeStruct((m, d), F32),
        compiler_params=_cparams(("parallel",)),
        name="out_proj",
    )(*mixed, w_out, x, g_post.reshape(1, d))


def _strict_upper_ones(n):
    r = lax.broadcasted_iota(jnp.int32, (n, n), 0)
    c = lax.broadcasted_iota(jnp.int32, (n, n), 1)
    return jnp.where(r > c, 1.0, 0.0).astype(BF16)


def _reverse_cumsum_groups(lk, run, tri):
    gw = tri.shape[0]
    ng = lk.shape[1] // gw
    rows = lk.shape[0]
    groups = [lk[:, gi * gw:(gi + 1) * gw] for gi in range(ng)]
    hi, lo = _split2(jnp.concatenate(groups, axis=0))
    local = _dot(hi, tri) + _dot(lo, tri)
    pieces = [None] * ng
    for gi in range(ng - 1, -1, -1):
        pieces[gi] = local[gi * rows:(gi + 1) * rows] + run
        run = run + jnp.sum(groups[gi], axis=-1, keepdims=True)
    return jnp.concatenate(pieces, axis=1), run


def _sb_prompt_kernel(q_ref, k_ref, v_ref, g_ref, o_ref, kb_ref, vb_ref, *, tq, t):
    h = pl.program_id(1)
    qi = pl.program_id(2)

    @pl.when(qi == 0)
    def _():
        for n in range(t // tq):
            rows = _head_rows(n * tq * N_HEADS + h, tq)
            kb_ref[n * tq:(n + 1) * tq, :] = k_ref[0, rows, :].astype(BF16)
            vb_ref[n * tq:(n + 1) * tq, :] = v_ref[0, rows, :].astype(BF16)

    q = q_ref[0]
    tri = _strict_upper_ones(HEAD_DIM)

    def block(kb, run, diagonal):
        k0 = pl.multiple_of(kb * tq, tq)
        z = _dot_nt(q, kb_ref[pl.ds(k0, tq), :]) * SCALE
        lk = -_softplus(z)
        if diagonal:
            mask = (lax.broadcasted_iota(jnp.int32, (tq, tq), 1)
                    < lax.broadcasted_iota(jnp.int32, (tq, tq), 0))
            lk = jnp.where(mask, lk, 0.0)
        between, run = _reverse_cumsum_groups(lk, run, tri)
        w = jnp.exp(z + lk + between)
        if diagonal:
            w = jnp.where(mask, w, 0.0)
        return _dot(w.astype(BF16), vb_ref[pl.ds(k0, tq), :]), run

    acc, run = block(qi, jnp.zeros((tq, 1), F32), True)

    def body(i, carry):
        acc, run = carry
        part, run = block(qi - 1 - i, run, False)
        return acc + part, run

    acc, _ = lax.fori_loop(0, qi, body, (acc, run))
    o_ref[...] = (acc * _silu(g_ref[0].astype(F32))).astype(o_ref.dtype)


def _sb_prompt(kv, lo, layer, b, t, tq=512):
    tq = min(tq, t)
    nq = t // tq
    qspec = lambda part: pl.BlockSpec((1, tq, HEAD_DIM), lambda bi, h, i: (part, bi * nq + i, h))
    kvspec = pl.BlockSpec((1, t * N_HEADS, HEAD_DIM), lambda bi, h, i: (layer, bi, 0))
    return pl.pallas_call(
        functools.partial(_sb_prompt_kernel, tq=tq, t=t),
        grid=(b, N_HEADS, nq),
        in_specs=[qspec(LO_SB_Q), kvspec, kvspec, qspec(LO_SB_G)],
        out_specs=pl.BlockSpec((tq, HEAD_DIM), lambda bi, h, i: (bi * nq + i, h)),
        out_shape=jax.ShapeDtypeStruct((b * t, GROUP_WIDTH), BF16),
        scratch_shapes=[pltpu.VMEM((t, HEAD_DIM), BF16), pltpu.VMEM((t, HEAD_DIM), BF16)],
        compiler_params=_cparams(("parallel", "arbitrary", "arbitrary")),
        name="sb_prompt",
    )(lo, kv[KV_SB_K], kv[KV_SB_V], lo)


def _lower_bound_logs(lb_logits, layer):
    depth = lb_logits.shape[0]
    rows = [lb_logits[i:i + 1, :] for i in range(depth)]
    mx = functools.reduce(jnp.maximum, rows)
    es = [jnp.exp(r - mx) for r in rows]
    tot = functools.reduce(lambda a, c: a + c, es)
    ps = [e / tot for e in es]
    cs = ps[0]
    for i in range(1, layer + 1):
        cs = cs + ps[i]
    lb = jnp.maximum(cs - ps[0], 0.0)
    return jnp.log(lb), jnp.log1p(-lb)


def _log_forget(x, log_lb, log_1m_lb):
    a = log_lb
    bb = log_1m_lb - _softplus(-x)
    return jnp.maximum(a, bb) + jnp.log1p(jnp.exp(-jnp.abs(a - bb)))


def _col_bcast(row):
    n = row.shape[-1]
    return jnp.broadcast_to(row, (n, n)).T


def _head_rmsnorm(o, g_row):
    ms = jnp.mean(o * o, axis=-1, keepdims=True)
    return o * lax.rsqrt(ms + RMS_EPS) * g_row


def _hg_prompt_kernel(q_ref, f_ref, i_ref, g_ref, lb_ref, gn_ref, o_ref, s_out_ref, s_ref,
                      *, c, layer):
    ci = pl.program_id(2)
    nc = pl.num_programs(2)
    d = HEAD_DIM

    @pl.when(ci == 0)
    def _():
        s_ref[...] = jnp.zeros_like(s_ref)

    log_lb, log_1m_lb = _lower_bound_logs(lb_ref[...], layer)
    g = _log_forget(f_ref[0], log_lb, log_1m_lb)
    kk = 1.0 - jnp.exp(g)
    qs = q_ref[0] * SCALE
    v = i_ref[0]
    vb = v.astype(BF16)

    r = lax.broadcasted_iota(jnp.int32, (c, c), 0)
    cc = lax.broadcasted_iota(jnp.int32, (c, c), 1)
    lower = jnp.where(r >= cc, 1.0, 0.0).astype(BF16)
    g_hi, g_mid, g_lo = _split3(g)
    gc = _dot(lower, g_hi) + _dot(lower, g_mid) + _dot(lower, g_lo)
    g_last = gc[c - 1:c, :]

    s = s_ref[...]
    o = _dot((qs * jnp.exp(gc)).astype(BF16), s.astype(BF16))

    a = jnp.zeros((c, c), F32)
    blk = 8
    while blk < c:
        nb = c // blk
        gc3 = gc.reshape(nb, blk, d)
        g_end = gc3[:, blk - 1:blk, :]
        g_prev = jnp.concatenate([jnp.zeros((1, 1, d), F32), g_end[:nb - 1]], axis=0)
        qd = (qs.reshape(nb, blk, d) * jnp.exp(gc3 - g_prev)).reshape(c, d)
        kd = (kk.reshape(nb, blk, d) * jnp.exp(g_end - gc3)).reshape(c, d)
        sc = _dot_nt(qd.astype(BF16), kd.astype(BF16))
        rb = r // blk
        cb = cc // blk
        pair = jnp.logical_and(rb % 2 == 1, cb == rb - 1)
        a = jnp.where(pair, sc, a)
        blk *= 2
    o = o + _dot(a.astype(BF16), vb)

    n8 = c // 8
    gc8 = gc.reshape(n8, 8, d)
    q8 = qs.reshape(n8, 8, d)
    k8 = kk.reshape(n8, 8, d)
    v8 = v.reshape(n8, 8, d)
    sub = lax.broadcasted_iota(jnp.int32, (n8, 8, d), 1)
    od = jnp.zeros((n8, 8, d), F32)
    for j in range(8):
        diff = gc8[:, j:j + 1, :] - gc8
        e = jnp.exp(jnp.where(sub <= j, diff, NEG_INF))
        sc = jnp.sum(e * k8 * q8[:, j:j + 1, :], axis=-1, keepdims=True)
        oj = jnp.sum(sc * v8, axis=1, keepdims=True)
        od = jnp.where(sub == j, oj, od)
    o = o + od.reshape(c, d)

    kd = (kk * jnp.exp(g_last - gc)).astype(BF16)
    s_new = _col_bcast(jnp.exp(g_last)) * s + _dot_tn(kd, vb)
    s_ref[...] = s_new

    @pl.when(ci == nc - 1)
    def _():
        s_out_ref[0, 0] = s_new

    o_ref[...] = (_head_rmsnorm(o, gn_ref[...]) * _silu(g_ref[0].astype(F32))).astype(o_ref.dtype)


def _hg_prompt(misc, lo, lb_logits, g_hg, layer, b, t, c=256):
    c = min(c, t)
    nc = t // c
    depth = lb_logits.shape[0]
    spec = lambda part: pl.BlockSpec((1, c, HEAD_DIM), lambda bi, h, i: (part, bi * nc + i, h))
    return pl.pallas_call(
        functools.partial(_hg_prompt_kernel, c=c, layer=layer),
        grid=(b, N_HEADS, nc),
        in_specs=[spec(MISC_HG_Q), spec(MISC_HG_F), spec(MISC_HG_I), spec(LO_HG_G),
                  pl.BlockSpec((depth, HEAD_DIM), lambda bi, h, i: (0, h)),
                  pl.BlockSpec((1, HEAD_DIM), lambda bi, h, i: (0, h))],
        out_specs=[pl.BlockSpec((c, HEAD_DIM), lambda bi, h, i: (bi * nc + i, h)),
                   pl.BlockSpec((1, 1, HEAD_DIM, HEAD_DIM), lambda bi, h, i: (bi, h, 0, 0))],
        out_shape=[jax.ShapeDtypeStruct((b * t, GROUP_WIDTH), BF16),
                   jax.ShapeDtypeStruct((b, N_HEADS, HEAD_DIM, HEAD_DIM), F32)],
        scratch_shapes=[pltpu.VMEM((HEAD_DIM, HEAD_DIM), F32)],
        compiler_params=_cparams(("parallel", "parallel", "arbitrary")),
        name="hgrn_prompt",
    )(misc, misc, misc, lo, lb_logits, g_hg.reshape(1, GROUP_WIDTH))


def _conv_prompt_kernel(u_ref, b_ref, c_ref, g_ref, w_ref, o_ref, st_ref, carry_ref, *, tt):
    ti = pl.program_id(1)
    nt = pl.num_programs(1)

    @pl.when(ti == 0)
    def _():
        carry_ref[...] = jnp.zeros_like(carry_ref)

    z = c_ref[0].astype(F32) * u_ref[0].astype(F32)
    row = lax.broadcasted_iota(jnp.int32, z.shape, 0)
    p0 = carry_ref[0:1, :]
    p1 = carry_ref[1:2, :]
    z1 = jnp.where(row == 0, p1, pltpu.roll(z, 1, 0))
    z2 = jnp.where(row == 0, p0, jnp.where(row == 1, p1, pltpu.roll(z, 2, 0)))
    y = z2 * w_ref[0:1, :] + z1 * w_ref[1:2, :] + z * w_ref[2:3, :]
    o_ref[...] = (b_ref[0].astype(F32) * y * _silu(g_ref[0].astype(F32))).astype(o_ref.dtype)
    tail = z[tt - 2:tt, :]
    carry_ref[0:2, :] = tail

    @pl.when(ti == nt - 1)
    def _():
        st_ref[0] = tail


def _conv_prompt(lo, conv_w_l, b, t, tt=512):
    tt = min(tt, t)
    nt = t // tt
    spec = lambda part: pl.BlockSpec((1, tt, GROUP_WIDTH), lambda bi, i: (part, bi * nt + i, 0))
    return pl.pallas_call(
        functools.partial(_conv_prompt_kernel, tt=tt),
        grid=(b, nt),
        in_specs=[spec(LO_CV_U), spec(LO_CV_B), spec(LO_CV_C), spec(LO_CV_G),
                  pl.BlockSpec((CONV_WIDTH, GROUP_WIDTH), lambda bi, i: (0, 0))],
        out_specs=[pl.BlockSpec((tt, GROUP_WIDTH), lambda bi, i: (bi * nt + i, 0)),
                   pl.BlockSpec((1, CONV_WIDTH - 1, GROUP_WIDTH), lambda bi, i: (bi, 0, 0))],
        out_shape=[jax.ShapeDtypeStruct((b * t, GROUP_WIDTH), BF16),
                   jax.ShapeDtypeStruct((b, CONV_WIDTH - 1, GROUP_WIDTH), F32)],
        scratch_shapes=[pltpu.VMEM((8, GROUP_WIDTH), F32)],
        compiler_params=_cparams(("parallel", "arbitrary")),
        name="conv_prompt",
    )(lo, lo, lo, lo, conv_w_l)


def _lane_pick(mat, idx):
    lane = lax.broadcasted_iota(jnp.int32, mat.shape, 1)
    return jnp.max(jnp.where(lane == idx, mat, NEG_INF), axis=-1, keepdims=True)


def _moba_select_t(gate_t, n_past, n_blocks):
    row = lax.broadcasted_iota(jnp.int32, gate_t.shape, 0)
    gm = jnp.where(row < n_past, gate_t, NEG_INF)
    sel = jnp.full(gate_t.shape, NEG_INF, F32)
    for n in range(n_blocks):
        cur = gm[n:n + 1, :]
        beats = jnp.where(gm > cur, 1.0, jnp.where(jnp.logical_and(gm == cur, row < n), 1.0, 0.0))
        rank = jnp.sum(beats, axis=0, keepdims=True)
        ok = jnp.logical_and(rank < MOBA_TOPK, cur > NEG_INF)
        sel = jnp.where(jnp.logical_and(row == n, ok), 0.0, sel)
    return sel


def _mb_prompt_kernel(q_ref, k_ref, v_ref, g_ref, o_ref, kmean_ref, kb_ref, vt_ref, *, nb):
    h = pl.program_id(1)
    qi = pl.program_id(2)
    blk = MOBA_BLOCK

    @pl.when(qi == 0)
    def _():
        kmean_ref[...] = jnp.zeros_like(kmean_ref)
        for n in range(nb):
            rows = _head_rows(n * blk * N_HEADS + h, blk)
            kn = k_ref[0, rows, :]
            kmean_ref[n:n + 1, :] = jnp.mean(kn, axis=0, keepdims=True)
            kb_ref[n * blk:(n + 1) * blk, :] = kn.astype(BF16)
            vt_ref[:, n * blk:(n + 1) * blk] = v_ref[0, rows, :].T.astype(BF16)

    q = q_ref[0]
    qb = q.astype(BF16)
    gate_t = lax.dot_general(kmean_ref[...], q, (((1,), (1,)), ((), ())),
                             precision=lax.Precision.HIGHEST, preferred_element_type=F32)
    sel_t = _moba_select_t(gate_t, qi, nb)

    rel = (lax.broadcasted_iota(jnp.int32, (blk, blk), 0)
           - lax.broadcasted_iota(jnp.int32, (blk, blk), 1))

    def attend(n_blk):
        parts = []
        for j in range(n_blk):
            s = (_dot_nt(kb_ref[j * blk:(j + 1) * blk, :], qb) * SCALE
                 + jnp.where(j < qi, sel_t[j:j + 1, :], 0.0))
            parts.append(jnp.where(rel <= (qi - j) * blk, s, NEG_INF))
        m = functools.reduce(jnp.maximum, [jnp.max(s, axis=0, keepdims=True) for s in parts])
        ps = [jnp.exp(s - m) for s in parts]
        l = functools.reduce(lambda a, c: a + c, [jnp.sum(p, axis=0, keepdims=True) for p in ps])
        p_all = jnp.concatenate([p.astype(BF16) for p in ps], axis=0)
        acc = _dot(vt_ref[:, :n_blk * blk], p_all)
        o_ref[...] = ((acc / l).T * _silu(g_ref[0].astype(F32))).astype(o_ref.dtype)

    for var in range(-(-nb // 2)):
        pl.when(qi // 2 == var)(functools.partial(attend, min(nb, 2 * var + 2)))


def _mb_prompt(kv, misc, lo, layer, b, t):
    blk = MOBA_BLOCK
    nb = t // blk
    nb_pad = -(-nb // 8) * 8
    qspec = lambda part: pl.BlockSpec((1, blk, HEAD_DIM), lambda bi, h, i: (part, bi * nb + i, h))
    kvspec = pl.BlockSpec((1, t * N_HEADS, HEAD_DIM), lambda bi, h, i: (layer, bi, 0))
    return pl.pallas_call(
        functools.partial(_mb_prompt_kernel, nb=nb),
        grid=(b, N_HEADS, nb),
        in_specs=[qspec(MISC_MB_Q), kvspec, kvspec, qspec(LO_MB_G)],
        out_specs=pl.BlockSpec((blk, HEAD_DIM), lambda bi, h, i: (bi * nb + i, h)),
        out_shape=jax.ShapeDtypeStruct((b * t, GROUP_WIDTH), BF16),
        scratch_shapes=[pltpu.VMEM((nb_pad, HEAD_DIM), F32), pltpu.VMEM((t, HEAD_DIM), BF16),
                        pltpu.VMEM((HEAD_DIM, t), BF16)],
        compiler_params=_cparams(("parallel", "arbitrary", "arbitrary")),
        name="moba_prompt",
    )(misc, kv[KV_MB_K], kv[KV_MB_V], lo)


def _stack_heads(x):
    return jnp.concatenate([x[:, h * HEAD_DIM:(h + 1) * HEAD_DIM] for h in range(N_HEADS)], axis=0)


def _spread_row(x, j, t):
    return jnp.concatenate(
        [jnp.broadcast_to(x[j * N_HEADS + h:j * N_HEADS + h + 1, :], (t, HEAD_DIM))
         for h in range(N_HEADS)], axis=0)


def _store_heads(o_ref, acc, gate, t):
    for h in range(N_HEADS):
        sl = slice(h * HEAD_DIM, (h + 1) * HEAD_DIM)
        o_ref[0, :, sl] = acc[h * t:(h + 1) * t, :] * _silu(gate[:, sl])


def _own_head(rows, cols, t):
    r = lax.broadcasted_iota(jnp.int32, (rows, cols), 0)
    c = lax.broadcasted_iota(jnp.int32, (rows, cols), 1)
    return c % N_HEADS == r // t


def _page_specs(layer, n_steps, n_group, page_rows, reverse):
    specs = []
    for gi in range(n_group):
        if reverse:
            imap = lambda bi, s, pt, gi=gi: (layer, pt[bi, (n_steps - 1 - s) * n_group + gi], 0, 0)
        else:
            imap = lambda bi, s, pt, gi=gi: (layer, pt[bi, s * n_group + gi], 0, 0)
        specs.append(pl.BlockSpec((None, None, page_rows, HEAD_DIM), imap))
    return specs


def _tok_spec(part, t):
    return pl.BlockSpec((1, 1, t, GROUP_WIDTH), lambda bi, *_: (part, bi, 0, 0))


def _new_kv_spec(layer, t):
    return pl.BlockSpec((1, 1, t * N_HEADS, HEAD_DIM), lambda bi, *_: (layer, bi, 0, 0))


def _sb_decode_kernel(pt_ref, q_ref, kn_ref, vn_ref, g_ref, *rest, t, n_group):
    k_refs = rest[:n_group]
    v_refs = rest[n_group:2 * n_group]
    o_ref, acc_ref, run_ref, qr_ref = rest[2 * n_group:]
    s = pl.program_id(1)
    ns = pl.num_programs(1)
    pr = k_refs[0].shape[0]
    rows = N_HEADS * t

    @pl.when(s == 0)
    def _():
        qr = _stack_heads(q_ref[0, 0])
        qr_ref[...] = qr
        tq = lax.broadcasted_iota(jnp.int32, (rows, 1), 0) % t
        kn = kn_ref[0, 0]
        vn = vn_ref[0, 0]
        acc = jnp.zeros((rows, HEAD_DIM), F32)
        run = jnp.zeros((rows, 1), F32)
        for j in range(t - 1, -1, -1):
            z = jnp.sum(qr * _spread_row(kn, j, t), axis=-1, keepdims=True) * SCALE
            ok = tq > j
            lk = jnp.where(ok, -_softplus(z), 0.0)
            w = jnp.where(ok, jnp.exp(z + lk + run), 0.0)
            acc = acc + w * _spread_row(vn, j, t)
            run = run + lk
        acc_ref[...] = acc
        run_ref[...] = jnp.broadcast_to(run, run_ref.shape)

    qb = qr_ref[...].astype(BF16)
    tri = _strict_upper_ones(HEAD_DIM)
    own = _own_head(rows, n_group * pr, t)
    z = jnp.concatenate([_dot_nt(qb, k_refs[gi][...].astype(BF16)) for gi in range(n_group)],
                        axis=1) * SCALE
    lk = jnp.where(own, -_softplus(z), 0.0)
    between, run = _reverse_cumsum_groups(lk, run_ref[:, 0:1], tri)
    w = jnp.where(own, jnp.exp(z + lk + between), 0.0).astype(BF16)
    acc = acc_ref[...]
    for gi in range(n_group):
        acc = acc + _dot(w[:, gi * pr:(gi + 1) * pr], v_refs[gi][...].astype(BF16))
    acc_ref[...] = acc
    run_ref[...] = jnp.broadcast_to(run, run_ref.shape)

    @pl.when(s == ns - 1)
    def _():
        _store_heads(o_ref, acc, g_ref[0, 0], t)


def _sb_decode(kv4, lo4, cache_k, cache_v, page_table, layer, n_group=8):
    _, bd, t, _ = lo4.shape
    n_pages = page_table.shape[1]
    pr = cache_k.shape[2]
    n_group = min(n_group, n_pages)
    n_steps = n_pages // n_group
    rows = N_HEADS * t
    grid_spec = pltpu.PrefetchScalarGridSpec(
        num_scalar_prefetch=1,
        grid=(bd, n_steps),
        in_specs=[_tok_spec(LO_SB_Q, t), _new_kv_spec(layer, t), _new_kv_spec(layer, t), _tok_spec(LO_SB_G, t)]
        + _page_specs(layer, n_steps, n_group, pr, True)
        + _page_specs(layer, n_steps, n_group, pr, True),
        out_specs=pl.BlockSpec((1, t, GROUP_WIDTH), lambda bi, s, pt: (bi, 0, 0)),
        scratch_shapes=[pltpu.VMEM((rows, HEAD_DIM), F32),
                        pltpu.VMEM((rows, HEAD_DIM), F32),
                        pltpu.VMEM((rows, HEAD_DIM), F32)],
    )
    return pl.pallas_call(
        functools.partial(_sb_decode_kernel, t=t, n_group=n_group),
        grid_spec=grid_spec,
        out_shape=jax.ShapeDtypeStruct((bd, t, GROUP_WIDTH), F32),
        compiler_params=_cparams(("parallel", "arbitrary")),
        name="sb_decode",
    )(page_table, lo4, kv4[KV_SB_K], kv4[KV_SB_V], lo4, *([cache_k] * n_group), *([cache_v] * n_group))


def _mb_gate_kernel(pt_ref, q_ref, *rest, t, n_group):
    k_refs = rest[:n_group]
    gate_ref, logit_ref, qr_ref = rest[n_group:]
    s = pl.program_id(1)
    pr = k_refs[0].shape[0]
    per_blk = MOBA_BLOCK * N_HEADS // pr

    @pl.when(s == 0)
    def _():
        qr_ref[...] = _stack_heads(q_ref[0, 0])
        gate_ref[...] = jnp.full(gate_ref.shape, NEG_INF, F32)

    qr = qr_ref[...]
    qb = qr.astype(BF16)
    lane = lax.broadcasted_iota(jnp.int32, gate_ref.shape[1:], 1)
    gates = gate_ref[0]
    for bi in range(n_group // per_blk):
        ksum = jnp.zeros((8, HEAD_DIM), F32)
        for pi in range(per_blk):
            gi = bi * per_blk + pi
            k = k_refs[gi][...]
            ksum = ksum + jnp.sum(k.reshape(pr // 8, 8, HEAD_DIM), axis=0)
            logit_ref[0, :, gi * pr:(gi + 1) * pr] = _dot_nt(qb, k.astype(BF16))
        kmean = (ksum[0:N_HEADS] + ksum[N_HEADS:2 * N_HEADS]) * (1.0 / MOBA_BLOCK)
        kmean_rows = jnp.concatenate(
            [jnp.broadcast_to(kmean[h:h + 1], (t, HEAD_DIM)) for h in range(N_HEADS)], axis=0)
        col = jnp.sum(qr * kmean_rows, axis=-1, keepdims=True)
        gates = jnp.where(lane == s * (n_group // per_blk) + bi, col, gates)
    gate_ref[0] = gates


def _mb_attend_kernel(pt_ref, q_ref, kn_ref, vn_ref, g_ref, gate_ref, logit_ref, *rest,
                      t, n_group, n_past):
    v_refs = rest[:n_group]
    o_ref, acc_ref, m_ref, l_ref, sel_ref = rest[n_group:]
    s = pl.program_id(1)
    ns = pl.num_programs(1)
    pr = v_refs[0].shape[0]
    per_blk = MOBA_BLOCK * N_HEADS // pr
    rows = N_HEADS * t

    @pl.when(s == 0)
    def _():
        neg = lambda n: jnp.full((n, HEAD_DIM), NEG_INF, F32)
        nbp = -(-n_past // 8) * 8
        gate_t = jnp.concatenate([gate_ref[0], neg(HEAD_DIM - rows)], axis=0).T
        sel_t = _moba_select_t(gate_t[:nbp], n_past, n_past)
        sel_ref[...] = jnp.concatenate([sel_t, neg(HEAD_DIM - nbp)], axis=0).T[:rows]
        qr = _stack_heads(q_ref[0, 0])
        tq = lax.broadcasted_iota(jnp.int32, (rows, 1), 0) % t
        kn = kn_ref[0, 0]
        vn = vn_ref[0, 0]
        zs = []
        for j in range(t):
            z = jnp.sum(qr * _spread_row(kn, j, t), axis=-1, keepdims=True) * SCALE
            zs.append(jnp.where(tq >= j, z, NEG_INF))
        m = functools.reduce(jnp.maximum, zs)
        l = jnp.zeros((rows, 1), F32)
        acc = jnp.zeros((rows, HEAD_DIM), F32)
        for j in range(t):
            p = jnp.exp(zs[j] - m)
            l = l + p
            acc = acc + p * _spread_row(vn, j, t)
        acc_ref[...] = acc
        m_ref[...] = jnp.broadcast_to(m, m_ref.shape)
        l_ref[...] = jnp.broadcast_to(l, l_ref.shape)

    acc = acc_ref[...]
    m = m_ref[:, 0:1]
    l = l_ref[:, 0:1]
    sel = sel_ref[...]
    own = _own_head(rows, n_group * pr, t)
    n_blk = n_group // per_blk
    bias = jnp.concatenate(
        [jnp.broadcast_to(_lane_pick(sel, s * n_blk + bi), (rows, per_blk * pr)) for bi in range(n_blk)],
        axis=1)
    sc = jnp.where(own, logit_ref[0] * SCALE + bias, NEG_INF)
    m_new = jnp.maximum(m, jnp.max(sc, axis=-1, keepdims=True))
    alpha = jnp.exp(m - m_new)
    p = jnp.exp(sc - m_new)
    l = alpha * l + jnp.sum(p, axis=-1, keepdims=True)
    pb = p.astype(BF16)
    acc = alpha * acc
    for gi in range(n_group):
        acc = acc + _dot(pb[:, gi * pr:(gi + 1) * pr], v_refs[gi][...].astype(BF16))
    m = m_new
    acc_ref[...] = acc
    m_ref[...] = jnp.broadcast_to(m, m_ref.shape)
    l_ref[...] = jnp.broadcast_to(l, l_ref.shape)

    @pl.when(s == ns - 1)
    def _():
        _store_heads(o_ref, acc / l, g_ref[0, 0], t)


def _mb_decode(kv4, misc4, lo4, cache_k, cache_v, page_table, layer, n_group=8):
    _, bd, t, _ = lo4.shape
    n_pages = page_table.shape[1]
    pr = cache_k.shape[2]
    n_group = min(n_group, n_pages)
    n_steps = n_pages // n_group
    n_past = n_pages * pr // (MOBA_BLOCK * N_HEADS)
    rows = N_HEADS * t
    gate_spec = pl.BlockSpec((1, rows, HEAD_DIM), lambda bi, s, pt: (bi, 0, 0))
    logit_spec = pl.BlockSpec((1, rows, n_group * pr), lambda bi, s, pt: (bi, 0, s))

    gates, logits = pl.pallas_call(
        functools.partial(_mb_gate_kernel, t=t, n_group=n_group),
        grid_spec=pltpu.PrefetchScalarGridSpec(
            num_scalar_prefetch=1,
            grid=(bd, n_steps),
            in_specs=[_tok_spec(MISC_MB_Q, t)] + _page_specs(layer, n_steps, n_group, pr, False),
            out_specs=[gate_spec, logit_spec],
            scratch_shapes=[pltpu.VMEM((rows, HEAD_DIM), F32)],
        ),
        out_shape=[jax.ShapeDtypeStruct((bd, rows, HEAD_DIM), F32),
                   jax.ShapeDtypeStruct((bd, rows, n_pages * pr), F32)],
        compiler_params=_cparams(("parallel", "arbitrary")),
        name="moba_decode_gate",
    )(page_table, misc4, *([cache_k] * n_group))

    return pl.pallas_call(
        functools.partial(_mb_attend_kernel, t=t, n_group=n_group, n_past=n_past),
        grid_spec=pltpu.PrefetchScalarGridSpec(
            num_scalar_prefetch=1,
            grid=(bd, n_steps),
            in_specs=[_tok_spec(MISC_MB_Q, t), _new_kv_spec(layer, t), _new_kv_spec(layer, t),
                      _tok_spec(LO_MB_G, t), gate_spec, logit_spec]
            + _page_specs(layer, n_steps, n_group, pr, False),
            out_specs=pl.BlockSpec((1, t, GROUP_WIDTH), lambda bi, s, pt: (bi, 0, 0)),
            scratch_shapes=[pltpu.VMEM((rows, HEAD_DIM), F32),
                            pltpu.VMEM((rows, HEAD_DIM), F32),
                            pltpu.VMEM((rows, HEAD_DIM), F32),
                            pltpu.VMEM((rows, HEAD_DIM), F32)],
        ),
        out_shape=jax.ShapeDtypeStruct((bd, t, GROUP_WIDTH), F32),
        compiler_params=_cparams(("parallel", "arbitrary")),
        name="moba_decode_attend",
    )(page_table, misc4, kv4[KV_MB_K], kv4[KV_MB_V], lo4, gates, logits, *([cache_v] * n_group))


def _small_decode_kernel(hq_ref, hf_ref, hi_ref, hgate_ref, cu_ref, cb_ref, cc_ref, cg_ref,
                         s0_ref, cv0_ref, lb_ref, gn_ref, w_ref,
                         ohg_ref, s_out_ref, ocv_ref, cv_out_ref, *, t, layer):
    d = HEAD_DIM
    log_lb, log_1m_lb = _lower_bound_logs(lb_ref[...], layer)
    g_all = _log_forget(hf_ref[0, 0], log_lb, log_1m_lb)
    k_all = 1.0 - jnp.exp(g_all)
    q_all = hq_ref[0, 0] * SCALE
    v_all = hi_ref[0, 0]
    gate_all = hgate_ref[0, 0]
    gn = gn_ref[...]
    pad = jnp.zeros((8 - t % 8, d), F32) if t % 8 else None

    for h in range(N_HEADS):
        sl = slice(h * d, (h + 1) * d)
        g, kk, qs, v = g_all[:, sl], k_all[:, sl], q_all[:, sl], v_all[:, sl]
        s0 = s0_ref[0, h]
        gcs = []
        run = jnp.zeros((1, d), F32)
        for i in range(t):
            run = run + g[i:i + 1]
            gcs.append(run)
        qe = jnp.concatenate([qs[i:i + 1] * jnp.exp(gcs[i]) for i in range(t)], axis=0)
        if pad is not None:
            qe = jnp.concatenate([qe, pad], axis=0)
        inter = _dot(qe.astype(BF16), s0.astype(BF16))
        s_new = _col_bcast(jnp.exp(gcs[t - 1])) * s0
        for j in range(t):
            s_new = s_new + _col_bcast(kk[j:j + 1] * jnp.exp(gcs[t - 1] - gcs[j])) * v[j:j + 1]
        s_out_ref[0, h] = s_new
        for i in range(t):
            o = inter[i:i + 1]
            for j in range(i + 1):
                sc = jnp.sum(qs[i:i + 1] * kk[j:j + 1] * jnp.exp(gcs[i] - gcs[j]),
                             axis=-1, keepdims=True)
                o = o + sc * v[j:j + 1]
            o = _head_rmsnorm(o, gn[:, sl]) * _silu(gate_all[i:i + 1, sl])
            ohg_ref[0, i:i + 1, sl] = o

    z = cc_ref[0, 0] * cu_ref[0, 0]
    zz = [cv0_ref[0, j:j + 1, :] for j in range(CONV_WIDTH - 1)] + [z[i:i + 1] for i in range(t)]
    cb = cb_ref[0, 0]
    cg = cg_ref[0, 0]
    for i in range(t):
        y = zz[i] * w_ref[0:1, :]
        for j in range(1, CONV_WIDTH):
            y = y + zz[i + j] * w_ref[j:j + 1, :]
        ocv_ref[0, i:i + 1, :] = cb[i:i + 1] * y * _silu(cg[i:i + 1])
    for j in range(CONV_WIDTH - 1):
        cv_out_ref[0, j:j + 1, :] = zz[t + j]


def _small_decode(misc4, lo4, state_hgrn, state_conv, lb_logits, g_hg, conv_w, layer):
    _, bd, t, _ = lo4.shape
    depth = lb_logits.shape[0]
    d = HEAD_DIM
    return pl.pallas_call(
        functools.partial(_small_decode_kernel, t=t, layer=layer),
        grid=(bd,),
        in_specs=[_tok_spec(MISC_HG_Q, t), _tok_spec(MISC_HG_F, t), _tok_spec(MISC_HG_I, t),
                  _tok_spec(LO_HG_G, t), _tok_spec(LO_CV_U, t), _tok_spec(LO_CV_B, t),
                  _tok_spec(LO_CV_C, t), _tok_spec(LO_CV_G, t),
                  pl.BlockSpec((None, 1, N_HEADS, d, d), lambda bi: (layer, bi, 0, 0, 0)),
                  pl.BlockSpec((None, 1, CONV_WIDTH - 1, GROUP_WIDTH), lambda bi: (layer, bi, 0, 0)),
                  pl.BlockSpec((depth, GROUP_WIDTH), lambda bi: (0, 0)),
                  pl.BlockSpec((1, GROUP_WIDTH), lambda bi: (0, 0)),
                  pl.BlockSpec((None, CONV_WIDTH, GROUP_WIDTH), lambda bi: (layer, 0, 0))],
        out_specs=[pl.BlockSpec((1, t, GROUP_WIDTH), lambda bi: (bi, 0, 0)),
                   pl.BlockSpec((1, N_HEADS, d, d), lambda bi: (bi, 0, 0, 0)),
                   pl.BlockSpec((1, t, GROUP_WIDTH), lambda bi: (bi, 0, 0)),
                   pl.BlockSpec((1, CONV_WIDTH - 1, GROUP_WIDTH), lambda bi: (bi, 0, 0))],
        out_shape=[jax.ShapeDtypeStruct((bd, t, GROUP_WIDTH), F32),
                   jax.ShapeDtypeStruct((bd, N_HEADS, d, d), F32),
                   jax.ShapeDtypeStruct((bd, t, GROUP_WIDTH), F32),
                   jax.ShapeDtypeStruct((bd, CONV_WIDTH - 1, GROUP_WIDTH), F32)],
        compiler_params=_cparams(("parallel",)),
        name="hgrn_conv_decode",
    )(misc4, misc4, misc4, lo4, lo4, lo4, lo4, lo4,
      state_hgrn, state_conv, lb_logits, g_hg.reshape(1, GROUP_WIDTH), conv_w)


def _row_tile(m, pref):
    return pref if m % pref == 0 else m


def kernel(x_prompt, x_sample, cache_sb_k, cache_sb_v, cache_moba_k, cache_moba_v, state_hgrn, state_conv,
           page_table, w_in, w_out, norm_pre, norm_post, hgrn_out_norm, conv_w, hgrn_lb_logits):
    b, t, d = x_prompt.shape
    bd, td, _ = x_sample.shape
    depth = w_in.shape[0]
    n_phys, ps = cache_sb_k.shape[1], cache_sb_k.shape[2]
    past_len = page_table.shape[1] * ps
    assert past_len % MOBA_BLOCK == 0 and td <= MOBA_BLOCK and t % MOBA_BLOCK == 0

    mp, ms = b * t, bd * td
    tm_p, tm_s = _row_tile(mp, 1024), _row_tile(ms, 1024)
    tm_o = _row_tile(mp, 512)
    tab_p = _rope_tables(jnp.arange(t, dtype=jnp.int32))
    pos_s = past_len + jnp.arange(td, dtype=jnp.int32)
    tab_s = tuple(jnp.tile(a, (bd, 1)) for a in _rope_tables(pos_s))
    assert t % tm_p == 0 or tm_p % t == 0
    if tm_p > t:
        tab_p = tuple(jnp.tile(a, (tm_p // t, 1)) for a in tab_p)

    w_steps = jnp.concatenate(
        [w_in[:, :, p * GROUP_WIDTH:(p + 1) * GROUP_WIDTH] for p in STEP_PARTS], axis=2).astype(BF16)
    w_out_b = w_out.astype(BF16).reshape(depth, 4, GROUP_WIDTH, d)
    pool = lambda c: c.reshape(depth, n_phys, ps * N_HEADS, HEAD_DIM)
    c_sb_k, c_sb_v, c_mb_k, c_mb_v = pool(cache_sb_k), pool(cache_sb_v), pool(cache_moba_k), pool(cache_moba_v)

    xp = x_prompt.reshape(mp, d)
    xs = x_sample.reshape(ms, d)
    kv_p, kv_s = [], []
    states_p = [[] for _ in range(2)]
    states_s = [[] for _ in range(2)]
    for l in range(depth):
        hp = _prenorm(xp, norm_pre[l], tm_o)
        kv_p, misc, lo = _project(hp, w_steps[l], tab_p, tm_p, l, depth, kv_p, BF16)
        o_sb = _sb_prompt(kv_p, lo, l, b, t)
        o_hg, hg_new = _hg_prompt(misc, lo, hgrn_lb_logits, hgrn_out_norm[l], l, b, t)
        o_cv, cv_new = _conv_prompt(lo, conv_w[l], b, t)
        o_mb = _mb_prompt(kv_p, misc, lo, l, b, t)
        xp = _out_project((o_sb, o_hg, o_cv, o_mb), w_out_b[l], xp, norm_post[l], tm_o)
        states_p[0].append(hg_new)
        states_p[1].append(cv_new)

        hs = _prenorm(xs, norm_pre[l], tm_s)
        kv_s, misc_s, lo_s = _project(hs, w_steps[l], tab_s, tm_s, l, depth, kv_s, F32)
        kv4 = [a.reshape(depth, bd, td * N_HEADS, HEAD_DIM) for a in kv_s]
        misc4 = misc_s.reshape(N_MISC, bd, td, GROUP_WIDTH)
        lo4 = lo_s.reshape(N_LO, bd, td, GROUP_WIDTH)
        s_sb = _sb_decode(kv4, lo4, c_sb_k, c_sb_v, page_table, l)
        s_hg, hg_new_s, s_cv, cv_new_s = _small_decode(misc4, lo4, state_hgrn, state_conv, hgrn_lb_logits,
                                                       hgrn_out_norm[l], conv_w, l)
        s_mb = _mb_decode(kv4, misc4, lo4, c_mb_k, c_mb_v, page_table, l)
        mixed_s = tuple(a.reshape(ms, GROUP_WIDTH).astype(BF16) for a in (s_sb, s_hg, s_cv, s_mb))
        xs = _out_project(mixed_s, w_out_b[l], xs, norm_post[l], tm_s)
        states_s[0].append(hg_new_s)
        states_s[1].append(cv_new_s)

    heads_p = lambda a: a.reshape(depth, b, t, N_HEADS, HEAD_DIM)
    heads_s = lambda a: a.reshape(depth, bd, td, N_HEADS, HEAD_DIM)
    return (xp.reshape(b, t, d), xs.reshape(bd, td, d),
            *(heads_p(a) for a in kv_p), jnp.stack(states_p[0]), jnp.stack(states_p[1]),
            *(heads_s(a) for a in kv_s), jnp.stack(states_s[0]), jnp.stack(states_s[1]))
```

```python
import functools

import jax
import jax.numpy as jnp
from jax import lax
from jax.experimental import pallas as pl
from jax.experimental.pallas import tpu as pltpu

F32 = jnp.float32
BF16 = jnp.bfloat16

HEAD_DIM = 128
N_HEADS = 4
GROUP_WIDTH = N_HEADS * HEAD_DIM
N_PARTS = 16
CONV_WIDTH = 3
MOBA_BLOCK = 256
MOBA_TOPK = 3
ROPE_THETA = 500000.0
ROPE_HALF = HEAD_DIM // 8
RMS_EPS = 1e-6
SCALE = HEAD_DIM ** -0.5
NEG_INF = float("-inf")

(P_SB_Q, P_SB_K, P_SB_V, P_SB_G, P_HG_Q, P_HG_F, P_HG_I, P_HG_G,
 P_CV_U, P_CV_B, P_CV_C, P_CV_G, P_MB_Q, P_MB_K, P_MB_V, P_MB_G) = range(N_PARTS)

STEP_PARTS = (P_SB_K, P_SB_V, P_MB_K, P_MB_V,
              P_HG_F, P_HG_Q, P_HG_I, P_MB_Q,
              P_SB_Q, P_SB_G, P_HG_G, P_CV_U, P_CV_B, P_CV_C, P_CV_G, P_MB_G)
N_KV, N_MISC, N_LO = 4, 4, 8
KV_SB_K, KV_SB_V, KV_MB_K, KV_MB_V = range(N_KV)
MISC_HG_F, MISC_HG_Q, MISC_HG_I, MISC_MB_Q = range(N_MISC)
LO_SB_Q, LO_SB_G, LO_HG_G, LO_CV_U, LO_CV_B, LO_CV_C, LO_CV_G, LO_MB_G = range(N_LO)
STEP_MB_K = STEP_PARTS.index(P_MB_K)
STEP_MB_Q = STEP_PARTS.index(P_MB_Q)

VMEM_LIMIT = 48 * 1024 * 1024


def _cparams(sem):
    return pltpu.CompilerParams(dimension_semantics=sem, vmem_limit_bytes=VMEM_LIMIT)


def _softplus(z):
    return jnp.maximum(z, 0.0) + jnp.log(1.0 + jnp.exp(-jnp.abs(z)))


def _silu(g):
    return g / (1.0 + jnp.exp(-g))


def _split2(x):
    hi = x.astype(BF16)
    lo = (x - hi.astype(F32)).astype(BF16)
    return hi, lo


def _split3(x):
    hi = x.astype(BF16)
    r = x - hi.astype(F32)
    mid = r.astype(BF16)
    lo = (r - mid.astype(F32)).astype(BF16)
    return hi, mid, lo


def _dot(a, b):
    return jnp.dot(a, b, preferred_element_type=F32)


def _dot_nt(a, b):
    return lax.dot_general(a, b, (((1,), (1,)), ((), ())), preferred_element_type=F32)


def _dot_tn(a, b):
    return lax.dot_general(a, b, (((0,), (0,)), ((), ())), preferred_element_type=F32)


def _head_rows(start, n):
    return pl.ds(start, n, stride=N_HEADS)


def _prenorm_kernel(x_ref, g_ref, h_ref):
    x = x_ref[...]
    ms = jnp.mean(x * x, axis=-1, keepdims=True)
    h_ref[...] = (x * lax.rsqrt(ms + RMS_EPS) * g_ref[...]).astype(h_ref.dtype)


def _prenorm(x, g_pre, tm):
    m, d = x.shape
    return pl.pallas_call(
        _prenorm_kernel,
        grid=(m // tm,),
        in_specs=[pl.BlockSpec((tm, d), lambda i: (i, 0)), pl.BlockSpec((1, d), lambda i: (0, 0))],
        out_specs=pl.BlockSpec((tm, d), lambda i: (i, 0)),
        out_shape=jax.ShapeDtypeStruct((m, d), BF16),
        compiler_params=_cparams(("parallel",)),
        name="prenorm",
    )(x, g_pre.reshape(1, d))


def _proj_kernel(h_ref, w_ref, c_ref, s1_ref, s2_ref, *rest, tm, n_alias):
    kv_refs = rest[n_alias:n_alias + N_KV]
    misc_ref, lo_ref = rest[n_alias + N_KV:]
    n = pl.program_id(1)
    acc = _dot(h_ref[...], w_ref[...])

    def head(hh, rope):
        a = acc[:, hh * HEAD_DIM:(hh + 1) * HEAD_DIM]
        if rope:
            a = (a * c_ref[...]
                 + pltpu.roll(a, HEAD_DIM - ROPE_HALF, 1) * s1_ref[...]
                 + pltpu.roll(a, ROPE_HALF, 1) * s2_ref[...])
        return a

    for a in range(N_KV):
        @pl.when(n == a)
        def _(a=a):
            for hh in range(N_HEADS):
                kv_refs[a][0, _head_rows(hh, tm), :] = head(hh, a == STEP_MB_K)

    @pl.when(jnp.logical_and(jnp.logical_and(n >= N_KV, n < N_KV + N_MISC), n != STEP_MB_Q))
    def _():
        misc_ref[0] = acc

    @pl.when(n == STEP_MB_Q)
    def _():
        for hh in range(N_HEADS):
            misc_ref[0, :, hh * HEAD_DIM:(hh + 1) * HEAD_DIM] = head(hh, True)

    @pl.when(n >= N_KV + N_MISC)
    def _():
        lo_ref[0] = acc.astype(lo_ref.dtype)


def _rope_tables(pos):
    inv_freq = ROPE_THETA ** (-jnp.arange(ROPE_HALF, dtype=F32) / ROPE_HALF)
    ang = pos.astype(F32)[:, None] * inv_freq[None, :]
    cos, sin = jnp.cos(ang), jnp.sin(ang)
    n = pos.shape[0]
    rest = HEAD_DIM - 2 * ROPE_HALF
    c = jnp.concatenate([cos, cos, jnp.ones((n, rest), F32)], axis=1)
    s1 = jnp.concatenate([-sin, jnp.zeros((n, HEAD_DIM - ROPE_HALF), F32)], axis=1)
    s2 = jnp.concatenate([jnp.zeros((n, ROPE_HALF), F32), sin, jnp.zeros((n, rest), F32)], axis=1)
    return c, s1, s2


def _project(h, w_steps, tables, tm, layer, depth, kv_prev, lo_dtype):
    m, d = h.shape
    period_blocks = tables[0].shape[0] // tm
    tab_spec = pl.BlockSpec((tm, HEAD_DIM), lambda i, n: (i % period_blocks, 0))
    n_alias = len(kv_prev)
    clamp = lambda v, hi: jnp.minimum(jnp.maximum(v, 0), hi)
    kv_spec = pl.BlockSpec((1, tm * N_HEADS, HEAD_DIM), lambda i, n: (layer, i, 0))
    outs = pl.pallas_call(
        functools.partial(_proj_kernel, tm=tm, n_alias=n_alias),
        grid=(m // tm, N_PARTS),
        in_specs=[
            pl.BlockSpec((tm, d), lambda i, n: (i, 0)),
            pl.BlockSpec((d, GROUP_WIDTH), lambda i, n: (0, n)),
            tab_spec, tab_spec, tab_spec,
        ] + [pl.BlockSpec(memory_space=pl.ANY)] * n_alias,
        out_specs=[kv_spec] * N_KV + [
            pl.BlockSpec((1, tm, GROUP_WIDTH), lambda i, n: (clamp(n - N_KV, N_MISC - 1), i, 0)),
            pl.BlockSpec((1, tm, GROUP_WIDTH), lambda i, n: (clamp(n - N_KV - N_MISC, N_LO - 1), i, 0)),
        ],
        out_shape=[jax.ShapeDtypeStruct((depth, m * N_HEADS, HEAD_DIM), F32)] * N_KV + [
            jax.ShapeDtypeStruct((N_MISC, m, GROUP_WIDTH), F32),
            jax.ShapeDtypeStruct((N_LO, m, GROUP_WIDTH), lo_dtype),
        ],
        input_output_aliases={5 + a: a for a in range(n_alias)},
        compiler_params=_cparams(("parallel", "arbitrary")),
        name="in_proj",
    )(h, w_steps, *tables, *kv_prev)
    return list(outs[:N_KV]), outs[N_KV], outs[N_KV + 1]


def _out_kernel(a_ref, b_ref, c_ref, d_ref, w_ref, x_ref, g_ref, o_ref):
    acc = _dot(a_ref[...], w_ref[0])
    acc += _dot(b_ref[...], w_ref[1])
    acc += _dot(c_ref[...], w_ref[2])
    acc += _dot(d_ref[...], w_ref[3])
    ms = jnp.mean(acc * acc, axis=-1, keepdims=True)
    o_ref[...] = x_ref[...] + acc * lax.rsqrt(ms + RMS_EPS) * g_ref[...]


def _out_project(mixed, w_out, x, g_post, tm):
    m, d = x.shape
    mix_spec = pl.BlockSpec((tm, GROUP_WIDTH), lambda i: (i, 0))
    return pl.pallas_call(
        _out_kernel,
        grid=(m // tm,),
        in_specs=[mix_spec, mix_spec, mix_spec, mix_spec,
                  pl.BlockSpec((4, GROUP_WIDTH, d), lambda i: (0, 0, 0)),
                  pl.BlockSpec((tm, d), lambda i: (i, 0)),
                  pl.BlockSpec((1, d), lambda i: (0, 0))],
        out_specs=pl.BlockSpec((tm, d), lambda i: (i, 0)),
        out_shape=jax.ShapeDtypeStruct((m, d), F32),
        compiler_params=_cparams(("parallel",)),
        name="out_proj",
    )(*mixed, w_out, x, g_post.reshape(1, d))


def _strict_upper_ones(n):
    r = lax.broadcasted_iota(jnp.int32, (n, n), 0)
    c = lax.broadcasted_iota(jnp.int32, (n, n), 1)
    return jnp.where(r > c, 1.0, 0.0).astype(BF16)


def _reverse_cumsum_groups(lk, run, tri):
    gw = tri.shape[0]
    ng = lk.shape[1] // gw
    rows = lk.shape[0]
    groups = [lk[:, gi * gw:(gi + 1) * gw] for gi in range(ng)]
    hi, lo = _split2(jnp.concatenate(groups, axis=0))
    local = _dot(hi, tri) + _dot(lo, tri)
    pieces = [None] * ng
    for gi in range(ng - 1, -1, -1):
        pieces[gi] = local[gi * rows:(gi + 1) * rows] + run
        run = run + jnp.sum(groups[gi], axis=-1, keepdims=True)
    return jnp.concatenate(pieces, axis=1), run


def _sb_prompt_kernel(q_ref, k_ref, v_ref, g_ref, o_ref, kb_ref, vb_ref, *, tq, t):
    h = pl.program_id(1)
    qi = pl.program_id(2)

    @pl.when(qi == 0)
    def _():
        for n in range(t // tq):
            rows = _head_rows(n * tq * N_HEADS + h, tq)
            kb_ref[n * tq:(n + 1) * tq, :] = k_ref[0, rows, :].astype(BF16)
            vb_ref[n * tq:(n + 1) * tq, :] = v_ref[0, rows, :].astype(BF16)

    q = q_ref[0]
    tri = _strict_upper_ones(HEAD_DIM)

    def block(kb, run, diagonal):
        k0 = kb * tq
        z = _dot_nt(q, kb_ref[pl.ds(k0, tq), :]) * SCALE
        lk = -_softplus(z)
        if diagonal:
            mask = (lax.broadcasted_iota(jnp.int32, (tq, tq), 1)
                    < lax.broadcasted_iota(jnp.int32, (tq, tq), 0))
            lk = jnp.where(mask, lk, 0.0)
        between, run = _reverse_cumsum_groups(lk, run, tri)
        w = jnp.exp(z + lk + between)
        if diagonal:
            w = jnp.where(mask, w, 0.0)
        return _dot(w.astype(BF16), vb_ref[pl.ds(k0, tq), :]), run

    def tile(n_past):
        acc, run = block(n_past, jnp.zeros((tq, 1), F32), True)
        for kb in range(n_past - 1, -1, -1):
            part, run = block(kb, run, False)
            acc = acc + part
        o_ref[...] = (acc * _silu(g_ref[0].astype(F32))).astype(o_ref.dtype)

    for n_past in range(t // tq):
        pl.when(qi == n_past)(functools.partial(tile, n_past))


def _sb_prompt(kv, lo, layer, b, t, tq=512):
    tq = min(tq, t)
    nq = t // tq
    qspec = lambda part: pl.BlockSpec((1, tq, HEAD_DIM), lambda bi, h, i: (part, bi * nq + i, h))
    kvspec = pl.BlockSpec((1, t * N_HEADS, HEAD_DIM), lambda bi, h, i: (layer, bi, 0))
    return pl.pallas_call(
        functools.partial(_sb_prompt_kernel, tq=tq, t=t),
        grid=(b, N_HEADS, nq),
        in_specs=[qspec(LO_SB_Q), kvspec, kvspec, qspec(LO_SB_G)],
        out_specs=pl.BlockSpec((tq, HEAD_DIM), lambda bi, h, i: (bi * nq + i, h)),
        out_shape=jax.ShapeDtypeStruct((b * t, GROUP_WIDTH), BF16),
        scratch_shapes=[pltpu.VMEM((t, HEAD_DIM), BF16), pltpu.VMEM((t, HEAD_DIM), BF16)],
        compiler_params=_cparams(("parallel", "arbitrary", "arbitrary")),
        name="sb_prompt",
    )(lo, kv[KV_SB_K], kv[KV_SB_V], lo)


def _lower_bound_logs(lb_logits, layer):
    depth = lb_logits.shape[0]
    rows = [lb_logits[i:i + 1, :] for i in range(depth)]
    mx = functools.reduce(jnp.maximum, rows)
    es = [jnp.exp(r - mx) for r in rows]
    tot = functools.reduce(lambda a, c: a + c, es)
    ps = [e / tot for e in es]
    cs = ps[0]
    for i in range(1, layer + 1):
        cs = cs + ps[i]
    lb = jnp.maximum(cs - ps[0], 0.0)
    return jnp.log(lb), jnp.log1p(-lb)


def _log_forget(x, log_lb, log_1m_lb):
    a = log_lb
    bb = log_1m_lb - _softplus(-x)
    return jnp.maximum(a, bb) + jnp.log1p(jnp.exp(-jnp.abs(a - bb)))


def _col_bcast(row):
    n = row.shape[-1]
    return jnp.broadcast_to(row, (n, n)).T


def _head_rmsnorm(o, g_row):
    ms = jnp.mean(o * o, axis=-1, keepdims=True)
    return o * lax.rsqrt(ms + RMS_EPS) * g_row


def _hg_prompt_kernel(q_ref, f_ref, i_ref, g_ref, lb_ref, gn_ref, o_ref, s_out_ref, s_ref,
                      *, c, layer):
    ci = pl.program_id(2)
    nc = pl.num_programs(2)
    d = HEAD_DIM

    @pl.when(ci == 0)
    def _():
        s_ref[...] = jnp.zeros_like(s_ref)

    log_lb, log_1m_lb = _lower_bound_logs(lb_ref[...], layer)
    g = _log_forget(f_ref[0], log_lb, log_1m_lb)
    kk = 1.0 - jnp.exp(g)
    qs = q_ref[0] * SCALE
    v = i_ref[0]
    vb = v.astype(BF16)

    r = lax.broadcasted_iota(jnp.int32, (c, c), 0)
    cc = lax.broadcasted_iota(jnp.int32, (c, c), 1)
    lower = jnp.where(r >= cc, 1.0, 0.0).astype(BF16)
    g_hi, g_mid, g_lo = _split3(g)
    gc = _dot(lower, g_hi) + _dot(lower, g_mid) + _dot(lower, g_lo)
    g_last = gc[c - 1:c, :]

    s = s_ref[...]
    o = _dot((qs * jnp.exp(gc)).astype(BF16), s.astype(BF16))

    a = jnp.zeros((c, c), F32)
    blk = 8
    while blk < c:
        nb = c // blk
        gc3 = gc.reshape(nb, blk, d)
        g_end = gc3[:, blk - 1:blk, :]
        g_prev = jnp.concatenate([jnp.zeros((1, 1, d), F32), g_end[:nb - 1]], axis=0)
        qd = (qs.reshape(nb, blk, d) * jnp.exp(gc3 - g_prev)).reshape(c, d)
        kd = (kk.reshape(nb, blk, d) * jnp.exp(g_end - gc3)).reshape(c, d)
        sc = _dot_nt(qd.astype(BF16), kd.astype(BF16))
        rb = r // blk
        cb = cc // blk
        pair = jnp.logical_and(rb % 2 == 1, cb == rb - 1)
        a = jnp.where(pair, sc, a)
        blk *= 2
    o = o + _dot(a.astype(BF16), vb)

    n8 = c // 8
    gc8 = gc.reshape(n8, 8, d)
    q8 = qs.reshape(n8, 8, d)
    k8 = kk.reshape(n8, 8, d)
    v8 = v.reshape(n8, 8, d)
    sub = lax.broadcasted_iota(jnp.int32, (n8, 8, d), 1)
    od = jnp.zeros((n8, 8, d), F32)
    for j in range(8):
        diff = gc8[:, j:j + 1, :] - gc8
        e = jnp.exp(jnp.where(sub <= j, diff, NEG_INF))
        sc = jnp.sum(e * k8 * q8[:, j:j + 1, :], axis=-1, keepdims=True)
        oj = jnp.sum(sc * v8, axis=1, keepdims=True)
        od = jnp.where(sub == j, oj, od)
    o = o + od.reshape(c, d)

    kd = (kk * jnp.exp(g_last - gc)).astype(BF16)
    s_new = _col_bcast(jnp.exp(g_last)) * s + _dot_tn(kd, vb)
    s_ref[...] = s_new

    @pl.when(ci == nc - 1)
    def _():
        s_out_ref[0, 0] = s_new

    o_ref[...] = (_head_rmsnorm(o, gn_ref[...]) * _silu(g_ref[0].astype(F32))).astype(o_ref.dtype)


def _hg_prompt(misc, lo, lb_logits, g_hg, layer, b, t, c=256):
    c = min(c, t)
    nc = t // c
    depth = lb_logits.shape[0]
    spec = lambda part: pl.BlockSpec((1, c, HEAD_DIM), lambda bi, h, i: (part, bi * nc + i, h))
    return pl.pallas_call(
        functools.partial(_hg_prompt_kernel, c=c, layer=layer),
        grid=(b, N_HEADS, nc),
        in_specs=[spec(MISC_HG_Q), spec(MISC_HG_F), spec(MISC_HG_I), spec(LO_HG_G),
                  pl.BlockSpec((depth, HEAD_DIM), lambda bi, h, i: (0, h)),
                  pl.BlockSpec((1, HEAD_DIM), lambda bi, h, i: (0, h))],
        out_specs=[pl.BlockSpec((c, HEAD_DIM), lambda bi, h, i: (bi * nc + i, h)),
                   pl.BlockSpec((1, 1, HEAD_DIM, HEAD_DIM), lambda bi, h, i: (bi, h, 0, 0))],
        out_shape=[jax.ShapeDtypeStruct((b * t, GROUP_WIDTH), BF16),
                   jax.ShapeDtypeStruct((b, N_HEADS, HEAD_DIM, HEAD_DIM), F32)],
        scratch_shapes=[pltpu.VMEM((HEAD_DIM, HEAD_DIM), F32)],
        compiler_params=_cparams(("parallel", "parallel", "arbitrary")),
        name="hgrn_prompt",
    )(misc, misc, misc, lo, lb_logits, g_hg.reshape(1, GROUP_WIDTH))


def _conv_prompt_kernel(u_ref, b_ref, c_ref, g_ref, w_ref, o_ref, st_ref, carry_ref, *, tt):
    ti = pl.program_id(1)
    nt = pl.num_programs(1)

    @pl.when(ti == 0)
    def _():
        carry_ref[...] = jnp.zeros_like(carry_ref)

    z = c_ref[0].astype(F32) * u_ref[0].astype(F32)
    row = lax.broadcasted_iota(jnp.int32, z.shape, 0)
    p0 = carry_ref[0:1, :]
    p1 = carry_ref[1:2, :]
    z1 = jnp.where(row == 0, p1, pltpu.roll(z, 1, 0))
    z2 = jnp.where(row == 0, p0, jnp.where(row == 1, p1, pltpu.roll(z, 2, 0)))
    y = z2 * w_ref[0:1, :] + z1 * w_ref[1:2, :] + z * w_ref[2:3, :]
    o_ref[...] = (b_ref[0].astype(F32) * y * _silu(g_ref[0].astype(F32))).astype(o_ref.dtype)
    tail = z[tt - 2:tt, :]
    carry_ref[0:2, :] = tail

    @pl.when(ti == nt - 1)
    def _():
        st_ref[0] = tail


def _conv_prompt(lo, conv_w_l, b, t, tt=512):
    tt = min(tt, t)
    nt = t // tt
    spec = lambda part: pl.BlockSpec((1, tt, GROUP_WIDTH), lambda bi, i: (part, bi * nt + i, 0))
    return pl.pallas_call(
        functools.partial(_conv_prompt_kernel, tt=tt),
        grid=(b, nt),
        in_specs=[spec(LO_CV_U), spec(LO_CV_B), spec(LO_CV_C), spec(LO_CV_G),
                  pl.BlockSpec((CONV_WIDTH, GROUP_WIDTH), lambda bi, i: (0, 0))],
        out_specs=[pl.BlockSpec((tt, GROUP_WIDTH), lambda bi, i: (bi * nt + i, 0)),
                   pl.BlockSpec((1, CONV_WIDTH - 1, GROUP_WIDTH), lambda bi, i: (bi, 0, 0))],
        out_shape=[jax.ShapeDtypeStruct((b * t, GROUP_WIDTH), BF16),
                   jax.ShapeDtypeStruct((b, CONV_WIDTH - 1, GROUP_WIDTH), F32)],
        scratch_shapes=[pltpu.VMEM((8, GROUP_WIDTH), F32)],
        compiler_params=_cparams(("parallel", "arbitrary")),
        name="conv_prompt",
    )(lo, lo, lo, lo, conv_w_l)


def _lane_pick(mat, idx):
    lane = lax.broadcasted_iota(jnp.int32, mat.shape, 1)
    return jnp.max(jnp.where(lane == idx, mat, NEG_INF), axis=-1, keepdims=True)


def _moba_select_t(gate_t, n_past, n_blocks):
    row = lax.broadcasted_iota(jnp.int32, gate_t.shape, 0)
    gm = jnp.where(row < n_past, gate_t, NEG_INF)
    sel = jnp.full(gate_t.shape, NEG_INF, F32)
    for n in range(n_blocks):
        cur = gm[n:n + 1, :]
        beats = jnp.where(gm > cur, 1.0, jnp.where(jnp.logical_and(gm == cur, row < n), 1.0, 0.0))
        rank = jnp.sum(beats, axis=0, keepdims=True)
        ok = jnp.logical_and(rank < MOBA_TOPK, cur > NEG_INF)
        sel = jnp.where(jnp.logical_and(row == n, ok), 0.0, sel)
    return sel


def _mb_prompt_kernel(q_ref, k_ref, v_ref, g_ref, o_ref, kmean_ref, kb_ref, vt_ref, *, nb, qt):
    h = pl.program_id(1)
    qi = pl.program_id(2)
    blk = MOBA_BLOCK

    @pl.when(qi == 0)
    def _():
        kmean_ref[...] = jnp.zeros_like(kmean_ref)
        for n in range(nb):
            rows = _head_rows(n * blk * N_HEADS + h, blk)
            kn = k_ref[0, rows, :]
            kmean_ref[n:n + 1, :] = jnp.mean(kn, axis=0, keepdims=True)
            kb_ref[n * blk:(n + 1) * blk, :] = kn.astype(BF16)
            vt_ref[:, n * blk:(n + 1) * blk] = v_ref[0, rows, :].T.astype(BF16)

    q = q_ref[0]
    qb = q.astype(BF16)
    per = qt // blk
    gate_t = lax.dot_general(kmean_ref[...], q, (((1,), (1,)), ((), ())),
                             precision=lax.Precision.HIGHEST, preferred_element_type=F32)
    own = qi * per + lax.broadcasted_iota(jnp.int32, (1, qt), 1) // blk
    sel_t = _moba_select_t(gate_t, own, nb)

    rel = (lax.broadcasted_iota(jnp.int32, (blk, qt), 0)
           - lax.broadcasted_iota(jnp.int32, (blk, qt), 1))

    def attend(n_blk):
        parts = []
        for j in range(n_blk):
            s = (_dot_nt(kb_ref[j * blk:(j + 1) * blk, :], qb) * SCALE
                 + jnp.where(j < own, sel_t[j:j + 1, :], 0.0))
            parts.append(jnp.where(rel <= (qi * per - j) * blk, s, NEG_INF))
        m = functools.reduce(jnp.maximum, [jnp.max(s, axis=0, keepdims=True) for s in parts])
        ps = [jnp.exp(s - m) for s in parts]
        l = functools.reduce(lambda a, c: a + c, [jnp.sum(p, axis=0, keepdims=True) for p in ps])
        p_all = jnp.concatenate([p.astype(BF16) for p in ps], axis=0)
        acc = _dot(vt_ref[:, :n_blk * blk], p_all)
        o_ref[...] = ((acc / l).T * _silu(g_ref[0].astype(F32))).astype(o_ref.dtype)

    for tile in range(nb // per):
        pl.when(qi == tile)(functools.partial(attend, per * (tile + 1)))


def _mb_prompt(kv, misc, lo, layer, b, t, qt=512):
    blk = MOBA_BLOCK
    nb = t // blk
    nb_pad = -(-nb // 8) * 8
    qt = min(qt, t)
    nq = t // qt
    qspec = lambda part: pl.BlockSpec((1, qt, HEAD_DIM), lambda bi, h, i: (part, bi * nq + i, h))
    kvspec = pl.BlockSpec((1, t * N_HEADS, HEAD_DIM), lambda bi, h, i: (layer, bi, 0))
    return pl.pallas_call(
        functools.partial(_mb_prompt_kernel, nb=nb, qt=qt),
        grid=(b, N_HEADS, nq),
        in_specs=[qspec(MISC_MB_Q), kvspec, kvspec, qspec(LO_MB_G)],
        out_specs=pl.BlockSpec((qt, HEAD_DIM), lambda bi, h, i: (bi * nq + i, h)),
        out_shape=jax.ShapeDtypeStruct((b * t, GROUP_WIDTH), BF16),
        scratch_shapes=[pltpu.VMEM((nb_pad, HEAD_DIM), F32), pltpu.VMEM((t, HEAD_DIM), BF16),
                        pltpu.VMEM((HEAD_DIM, t), BF16)],
        compiler_params=_cparams(("parallel", "arbitrary", "arbitrary")),
        name="moba_prompt",
    )(misc, kv[KV_MB_K], kv[KV_MB_V], lo)


def _stack_heads(x):
    return jnp.concatenate([x[:, h * HEAD_DIM:(h + 1) * HEAD_DIM] for h in range(N_HEADS)], axis=0)


def _spread_row(x, j, t):
    return jnp.concatenate(
        [jnp.broadcast_to(x[j * N_HEADS + h:j * N_HEADS + h + 1, :], (t, HEAD_DIM))
         for h in range(N_HEADS)], axis=0)


def _store_heads(o_ref, acc, gate, t):
    for h in range(N_HEADS):
        sl = slice(h * HEAD_DIM, (h + 1) * HEAD_DIM)
        o_ref[0, :, sl] = acc[h * t:(h + 1) * t, :] * _silu(gate[:, sl])


def _own_head(rows, cols, t):
    r = lax.broadcasted_iota(jnp.int32, (rows, cols), 0)
    c = lax.broadcasted_iota(jnp.int32, (rows, cols), 1)
    return c % N_HEADS == r // t


def _page_specs(layer, n_steps, n_group, page_rows, reverse):
    specs = []
    for gi in range(n_group):
        if reverse:
            imap = lambda bi, s, pt, gi=gi: (layer, pt[bi, (n_steps - 1 - s) * n_group + gi], 0, 0)
        else:
            imap = lambda bi, s, pt, gi=gi: (layer, pt[bi, s * n_group + gi], 0, 0)
        specs.append(pl.BlockSpec((None, None, page_rows, HEAD_DIM), imap))
    return specs


def _tok_spec(part, t):
    return pl.BlockSpec((1, 1, t, GROUP_WIDTH), lambda bi, *_: (part, bi, 0, 0))


def _new_kv_spec(layer, t):
    return pl.BlockSpec((1, 1, t * N_HEADS, HEAD_DIM), lambda bi, *_: (layer, bi, 0, 0))


def _sb_decode_kernel(pt_ref, q_ref, kn_ref, vn_ref, g_ref, *rest, t, n_group):
    k_refs = rest[:n_group]
    v_refs = rest[n_group:2 * n_group]
    o_ref, acc_ref, run_ref, qr_ref = rest[2 * n_group:]
    s = pl.program_id(1)
    ns = pl.num_programs(1)
    pr = k_refs[0].shape[0]
    rows = N_HEADS * t

    @pl.when(s == 0)
    def _():
        qr = _stack_heads(q_ref[0, 0])
        qr_ref[...] = qr
        tq = lax.broadcasted_iota(jnp.int32, (rows, 1), 0) % t
        kn = kn_ref[0, 0]
        vn = vn_ref[0, 0]
        acc = jnp.zeros((rows, HEAD_DIM), F32)
        run = jnp.zeros((rows, 1), F32)
        for j in range(t - 1, -1, -1):
            z = jnp.sum(qr * _spread_row(kn, j, t), axis=-1, keepdims=True) * SCALE
            ok = tq > j
            lk = jnp.where(ok, -_softplus(z), 0.0)
            w = jnp.where(ok, jnp.exp(z + lk + run), 0.0)
            acc = acc + w * _spread_row(vn, j, t)
            run = run + lk
        acc_ref[...] = acc
        run_ref[...] = jnp.broadcast_to(run, run_ref.shape)

    qb = qr_ref[...].astype(BF16)
    tri = _strict_upper_ones(HEAD_DIM)
    own = _own_head(rows, n_group * pr, t)
    z = jnp.concatenate([_dot_nt(qb, k_refs[gi][...].astype(BF16)) for gi in range(n_group)],
                        axis=1) * SCALE
    lk = jnp.where(own, -_softplus(z), 0.0)
    between, run = _reverse_cumsum_groups(lk, run_ref[:, 0:1], tri)
    w = jnp.where(own, jnp.exp(z + lk + between), 0.0).astype(BF16)
    acc = acc_ref[...]
    for gi in range(n_group):
        acc = acc + _dot(w[:, gi * pr:(gi + 1) * pr], v_refs[gi][...].astype(BF16))
    acc_ref[...] = acc
    run_ref[...] = jnp.broadcast_to(run, run_ref.shape)

    @pl.when(s == ns - 1)
    def _():
        _store_heads(o_ref, acc, g_ref[0, 0], t)


def _sb_decode(kv4, lo4, cache_k, cache_v, page_table, layer, n_group=16):
    _, bd, t, _ = lo4.shape
    n_pages = page_table.shape[1]
    pr = cache_k.shape[2]
    n_group = min(n_group, n_pages)
    n_steps = n_pages // n_group
    rows = N_HEADS * t
    grid_spec = pltpu.PrefetchScalarGridSpec(
        num_scalar_prefetch=1,
        grid=(bd, n_steps),
        in_specs=[_tok_spec(LO_SB_Q, t), _new_kv_spec(layer, t), _new_kv_spec(layer, t), _tok_spec(LO_SB_G, t)]
        + _page_specs(layer, n_steps, n_group, pr, True)
        + _page_specs(layer, n_steps, n_group, pr, True),
        out_specs=pl.BlockSpec((1, t, GROUP_WIDTH), lambda bi, s, pt: (bi, 0, 0)),
        scratch_shapes=[pltpu.VMEM((rows, HEAD_DIM), F32),
                        pltpu.VMEM((rows, HEAD_DIM), F32),
                        pltpu.VMEM((rows, HEAD_DIM), F32)],
    )
    return pl.pallas_call(
        functools.partial(_sb_decode_kernel, t=t, n_group=n_group),
        grid_spec=grid_spec,
        out_shape=jax.ShapeDtypeStruct((bd, t, GROUP_WIDTH), F32),
        compiler_params=_cparams(("parallel", "arbitrary")),
        name="sb_decode",
    )(page_table, lo4, kv4[KV_SB_K], kv4[KV_SB_V], lo4, *([cache_k] * n_group), *([cache_v] * n_group))


def _mb_gate_kernel(pt_ref, q_ref, *rest, t, n_group):
    k_refs = rest[:n_group]
    gate_ref, logit_ref, qr_ref = rest[n_group:]
    s = pl.program_id(1)
    pr = k_refs[0].shape[0]
    per_blk = MOBA_BLOCK * N_HEADS // pr

    @pl.when(s == 0)
    def _():
        qr_ref[...] = _stack_heads(q_ref[0, 0])
        gate_ref[...] = jnp.full(gate_ref.shape, NEG_INF, F32)

    qr = qr_ref[...]
    qb = qr.astype(BF16)
    lane = lax.broadcasted_iota(jnp.int32, gate_ref.shape[1:], 1)
    gates = gate_ref[0]
    for bi in range(n_group // per_blk):
        ksum = jnp.zeros((8, HEAD_DIM), F32)
        for pi in range(per_blk):
            gi = bi * per_blk + pi
            k = k_refs[gi][...]
            ksum = ksum + jnp.sum(k.reshape(pr // 8, 8, HEAD_DIM), axis=0)
            logit_ref[0, :, gi * pr:(gi + 1) * pr] = _dot_nt(qb, k.astype(BF16))
        kmean = (ksum[0:N_HEADS] + ksum[N_HEADS:2 * N_HEADS]) * (1.0 / MOBA_BLOCK)
        kmean_rows = jnp.concatenate(
            [jnp.broadcast_to(kmean[h:h + 1], (t, HEAD_DIM)) for h in range(N_HEADS)], axis=0)
        col = jnp.sum(qr * kmean_rows, axis=-1, keepdims=True)
        gates = jnp.where(lane == s * (n_group // per_blk) + bi, col, gates)
    gate_ref[0] = gates


def _mb_attend_kernel(pt_ref, q_ref, kn_ref, vn_ref, g_ref, gate_ref, logit_ref, *rest,
                      t, n_group, n_past):
    v_refs = rest[:n_group]
    o_ref, acc_ref, m_ref, l_ref, sel_ref = rest[n_group:]
    s = pl.program_id(1)
    ns = pl.num_programs(1)
    pr = v_refs[0].shape[0]
    per_blk = MOBA_BLOCK * N_HEADS // pr
    rows = N_HEADS * t

    @pl.when(s == 0)
    def _():
        neg = lambda n: jnp.full((n, HEAD_DIM), NEG_INF, F32)
        nbp = -(-n_past // 8) * 8
        gate_t = jnp.concatenate([gate_ref[0], neg(HEAD_DIM - rows)], axis=0).T
        sel_t = _moba_select_t(gate_t[:nbp], n_past, n_past)
        sel_ref[...] = jnp.concatenate([sel_t, neg(HEAD_DIM - nbp)], axis=0).T[:rows]
        qr = _stack_heads(q_ref[0, 0])
        tq = lax.broadcasted_iota(jnp.int32, (rows, 1), 0) % t
        kn = kn_ref[0, 0]
        vn = vn_ref[0, 0]
        zs = []
        for j in range(t):
            z = jnp.sum(qr * _spread_row(kn, j, t), axis=-1, keepdims=True) * SCALE
            zs.append(jnp.where(tq >= j, z, NEG_INF))
        m = functools.reduce(jnp.maximum, zs)
        l = jnp.zeros((rows, 1), F32)
        acc = jnp.zeros((rows, HEAD_DIM), F32)
        for j in range(t):
            p = jnp.exp(zs[j] - m)
            l = l + p
            acc = acc + p * _spread_row(vn, j, t)
        acc_ref[...] = acc
        m_ref[...] = jnp.broadcast_to(m, m_ref.shape)
        l_ref[...] = jnp.broadcast_to(l, l_ref.shape)

    acc = acc_ref[...]
    m = m_ref[:, 0:1]
    l = l_ref[:, 0:1]
    sel = sel_ref[...]
    own = _own_head(rows, n_group * pr, t)
    n_blk = n_group // per_blk
    bias = jnp.concatenate(
        [jnp.broadcast_to(_lane_pick(sel, s * n_blk + bi), (rows, per_blk * pr)) for bi in range(n_blk)],
        axis=1)
    sc = jnp.where(own, logit_ref[0] * SCALE + bias, NEG_INF)
    m_new = jnp.maximum(m, jnp.max(sc, axis=-1, keepdims=True))
    alpha = jnp.exp(m - m_new)
    p = jnp.exp(sc - m_new)
    l = alpha * l + jnp.sum(p, axis=-1, keepdims=True)
    pb = p.astype(BF16)
    acc = alpha * acc
    for gi in range(n_group):
        acc = acc + _dot(pb[:, gi * pr:(gi + 1) * pr], v_refs[gi][...].astype(BF16))
    m = m_new
    acc_ref[...] = acc
    m_ref[...] = jnp.broadcast_to(m, m_ref.shape)
    l_ref[...] = jnp.broadcast_to(l, l_ref.shape)

    @pl.when(s == ns - 1)
    def _():
        _store_heads(o_ref, acc / l, g_ref[0, 0], t)


def _mb_decode(kv4, misc4, lo4, cache_k, cache_v, page_table, layer, n_group=16):
    _, bd, t, _ = lo4.shape
    n_pages = page_table.shape[1]
    pr = cache_k.shape[2]
    n_group = min(n_group, n_pages)
    n_steps = n_pages // n_group
    n_past = n_pages * pr // (MOBA_BLOCK * N_HEADS)
    rows = N_HEADS * t
    gate_spec = pl.BlockSpec((1, rows, HEAD_DIM), lambda bi, s, pt: (bi, 0, 0))
    logit_spec = pl.BlockSpec((1, rows, n_group * pr), lambda bi, s, pt: (bi, 0, s))

    gates, logits = pl.pallas_call(
        functools.partial(_mb_gate_kernel, t=t, n_group=n_group),
        grid_spec=pltpu.PrefetchScalarGridSpec(
            num_scalar_prefetch=1,
            grid=(bd, n_steps),
            in_specs=[_tok_spec(MISC_MB_Q, t)] + _page_specs(layer, n_steps, n_group, pr, False),
            out_specs=[gate_spec, logit_spec],
            scratch_shapes=[pltpu.VMEM((rows, HEAD_DIM), F32)],
        ),
        out_shape=[jax.ShapeDtypeStruct((bd, rows, HEAD_DIM), F32),
                   jax.ShapeDtypeStruct((bd, rows, n_pages * pr), F32)],
        compiler_params=_cparams(("parallel", "arbitrary")),
        name="moba_decode_gate",
    )(page_table, misc4, *([cache_k] * n_group))

    return pl.pallas_call(
        functools.partial(_mb_attend_kernel, t=t, n_group=n_group, n_past=n_past),
        grid_spec=pltpu.PrefetchScalarGridSpec(
            num_scalar_prefetch=1,
            grid=(bd, n_steps),
            in_specs=[_tok_spec(MISC_MB_Q, t), _new_kv_spec(layer, t), _new_kv_spec(layer, t),
                      _tok_spec(LO_MB_G, t), gate_spec, logit_spec]
            + _page_specs(layer, n_steps, n_group, pr, False),
            out_specs=pl.BlockSpec((1, t, GROUP_WIDTH), lambda bi, s, pt: (bi, 0, 0)),
            scratch_shapes=[pltpu.VMEM((rows, HEAD_DIM), F32),
                            pltpu.VMEM((rows, HEAD_DIM), F32),
                            pltpu.VMEM((rows, HEAD_DIM), F32),
                            pltpu.VMEM((rows, HEAD_DIM), F32)],
        ),
        out_shape=jax.ShapeDtypeStruct((bd, t, GROUP_WIDTH), F32),
        compiler_params=_cparams(("parallel", "arbitrary")),
        name="moba_decode_attend",
    )(page_table, misc4, kv4[KV_MB_K], kv4[KV_MB_V], lo4, gates, logits, *([cache_v] * n_group))


def _small_decode_kernel(hq_ref, hf_ref, hi_ref, hgate_ref, cu_ref, cb_ref, cc_ref, cg_ref,
                         s0_ref, cv0_ref, lb_ref, gn_ref, w_ref,
                         ohg_ref, s_out_ref, ocv_ref, cv_out_ref, *, t, layer):
    d = HEAD_DIM
    log_lb, log_1m_lb = _lower_bound_logs(lb_ref[...], layer)
    g_all = _log_forget(hf_ref[0, 0], log_lb, log_1m_lb)
    k_all = 1.0 - jnp.exp(g_all)
    q_all = hq_ref[0, 0] * SCALE
    v_all = hi_ref[0, 0]
    gate_all = hgate_ref[0, 0]
    gn = gn_ref[...]
    pad = jnp.zeros((8 - t % 8, d), F32) if t % 8 else None

    for h in range(N_HEADS):
        sl = slice(h * d, (h + 1) * d)
        g, kk, qs, v = g_all[:, sl], k_all[:, sl], q_all[:, sl], v_all[:, sl]
        s0 = s0_ref[0, h]
        gcs = []
        run = jnp.zeros((1, d), F32)
        for i in range(t):
            run = run + g[i:i + 1]
            gcs.append(run)
        qe = jnp.concatenate([qs[i:i + 1] * jnp.exp(gcs[i]) for i in range(t)], axis=0)
        if pad is not None:
            qe = jnp.concatenate([qe, pad], axis=0)
        inter = _dot(qe.astype(BF16), s0.astype(BF16))
        s_new = _col_bcast(jnp.exp(gcs[t - 1])) * s0
        for j in range(t):
            s_new = s_new + _col_bcast(kk[j:j + 1] * jnp.exp(gcs[t - 1] - gcs[j])) * v[j:j + 1]
        s_out_ref[0, h] = s_new
        for i in range(t):
            o = inter[i:i + 1]
            for j in range(i + 1):
                sc = jnp.sum(qs[i:i + 1] * kk[j:j + 1] * jnp.exp(gcs[i] - gcs[j]),
                             axis=-1, keepdims=True)
                o = o + sc * v[j:j + 1]
            o = _head_rmsnorm(o, gn[:, sl]) * _silu(gate_all[i:i + 1, sl])
            ohg_ref[0, i:i + 1, sl] = o

    z = cc_ref[0, 0] * cu_ref[0, 0]
    zz = [cv0_ref[0, j:j + 1, :] for j in range(CONV_WIDTH - 1)] + [z[i:i + 1] for i in range(t)]
    cb = cb_ref[0, 0]
    cg = cg_ref[0, 0]
    for i in range(t):
        y = zz[i] * w_ref[0:1, :]
        for j in range(1, CONV_WIDTH):
            y = y + zz[i + j] * w_ref[j:j + 1, :]
        ocv_ref[0, i:i + 1, :] = cb[i:i + 1] * y * _silu(cg[i:i + 1])
    for j in range(CONV_WIDTH - 1):
        cv_out_ref[0, j:j + 1, :] = zz[t + j]


def _small_decode(misc4, lo4, state_hgrn, state_conv, lb_logits, g_hg, conv_w, layer):
    _, bd, t, _ = lo4.shape
    depth = lb_logits.shape[0]
    d = HEAD_DIM
    return pl.pallas_call(
        functools.partial(_small_decode_kernel, t=t, layer=layer),
        grid=(bd,),
        in_specs=[_tok_spec(MISC_HG_Q, t), _tok_spec(MISC_HG_F, t), _tok_spec(MISC_HG_I, t),
                  _tok_spec(LO_HG_G, t), _tok_spec(LO_CV_U, t), _tok_spec(LO_CV_B, t),
                  _tok_spec(LO_CV_C, t), _tok_spec(LO_CV_G, t),
                  pl.BlockSpec((None, 1, N_HEADS, d, d), lambda bi: (layer, bi, 0, 0, 0)),
                  pl.BlockSpec((None, 1, CONV_WIDTH - 1, GROUP_WIDTH), lambda bi: (layer, bi, 0, 0)),
                  pl.BlockSpec((depth, GROUP_WIDTH), lambda bi: (0, 0)),
                  pl.BlockSpec((1, GROUP_WIDTH), lambda bi: (0, 0)),
                  pl.BlockSpec((None, CONV_WIDTH, GROUP_WIDTH), lambda bi: (layer, 0, 0))],
        out_specs=[pl.BlockSpec((1, t, GROUP_WIDTH), lambda bi: (bi, 0, 0)),
                   pl.BlockSpec((1, N_HEADS, d, d), lambda bi: (bi, 0, 0, 0)),
                   pl.BlockSpec((1, t, GROUP_WIDTH), lambda bi: (bi, 0, 0)),
                   pl.BlockSpec((1, CONV_WIDTH - 1, GROUP_WIDTH), lambda bi: (bi, 0, 0))],
        out_shape=[jax.ShapeDtypeStruct((bd, t, GROUP_WIDTH), F32),
                   jax.ShapeDtypeStruct((bd, N_HEADS, d, d), F32),
                   jax.ShapeDtypeStruct((bd, t, GROUP_WIDTH), F32),
                   jax.ShapeDtypeStruct((bd, CONV_WIDTH - 1, GROUP_WIDTH), F32)],
        compiler_params=_cparams(("parallel",)),
        name="hgrn_conv_decode",
    )(misc4, misc4, misc4, lo4, lo4, lo4, lo4, lo4,
      state_hgrn, state_conv, lb_logits, g_hg.reshape(1, GROUP_WIDTH), conv_w)


def _row_tile(m, pref):
    return pref if m % pref == 0 else m


def kernel(x_prompt, x_sample, cache_sb_k, cache_sb_v, cache_moba_k, cache_moba_v, state_hgrn, state_conv,
           page_table, w_in, w_out, norm_pre, norm_post, hgrn_out_norm, conv_w, hgrn_lb_logits):
    b, t, d = x_prompt.shape
    bd, td, _ = x_sample.shape
    depth = w_in.shape[0]
    n_phys, ps = cache_sb_k.shape[1], cache_sb_k.shape[2]
    past_len = page_table.shape[1] * ps
    assert past_len % MOBA_BLOCK == 0 and td <= MOBA_BLOCK and t % MOBA_BLOCK == 0

    mp, ms = b * t, bd * td
    tm_p, tm_s = _row_tile(mp, 1024), _row_tile(ms, 1024)
    tm_o = _row_tile(mp, 512)
    tab_p = _rope_tables(jnp.arange(t, dtype=jnp.int32))
    pos_s = past_len + jnp.arange(td, dtype=jnp.int32)
    tab_s = tuple(jnp.tile(a, (bd, 1)) for a in _rope_tables(pos_s))
    assert t % tm_p == 0 or tm_p % t == 0
    if tm_p > t:
        tab_p = tuple(jnp.tile(a, (tm_p // t, 1)) for a in tab_p)

    w_steps = jnp.concatenate(
        [w_in[:, :, p * GROUP_WIDTH:(p + 1) * GROUP_WIDTH] for p in STEP_PARTS], axis=2).astype(BF16)
    w_out_b = w_out.astype(BF16).reshape(depth, 4, GROUP_WIDTH, d)
    pool = lambda c: c.reshape(depth, n_phys, ps * N_HEADS, HEAD_DIM)
    c_sb_k, c_sb_v, c_mb_k, c_mb_v = pool(cache_sb_k), pool(cache_sb_v), pool(cache_moba_k), pool(cache_moba_v)

    xp = x_prompt.reshape(mp, d)
    xs = x_sample.reshape(ms, d)
    kv_p, kv_s = [], []
    states_p = [[] for _ in range(2)]
    states_s = [[] for _ in range(2)]
    for l in range(depth):
        hp = _prenorm(xp, norm_pre[l], tm_o)
        kv_p, misc, lo = _project(hp, w_steps[l], tab_p, tm_p, l, depth, kv_p, BF16)
        o_sb = _sb_prompt(kv_p, lo, l, b, t)
        o_hg, hg_new = _hg_prompt(misc, lo, hgrn_lb_logits, hgrn_out_norm[l], l, b, t)
        o_cv, cv_new = _conv_prompt(lo, conv_w[l], b, t)
        o_mb = _mb_prompt(kv_p, misc, lo, l, b, t)
        xp = _out_project((o_sb, o_hg, o_cv, o_mb), w_out_b[l], xp, norm_post[l], tm_o)
        states_p[0].append(hg_new)
        states_p[1].append(cv_new)

        hs = _prenorm(xs, norm_pre[l], tm_s)
        kv_s, misc_s, lo_s = _project(hs, w_steps[l], tab_s, tm_s, l, depth, kv_s, F32)
        kv4 = [a.reshape(depth, bd, td * N_HEADS, HEAD_DIM) for a in kv_s]
        misc4 = misc_s.reshape(N_MISC, bd, td, GROUP_WIDTH)
        lo4 = lo_s.reshape(N_LO, bd, td, GROUP_WIDTH)
        s_sb = _sb_decode(kv4, lo4, c_sb_k, c_sb_v, page_table, l)
        s_hg, hg_new_s, s_cv, cv_new_s = _small_decode(misc4, lo4, state_hgrn, state_conv, hgrn_lb_logits,
                                                       hgrn_out_norm[l], conv_w, l)
        s_mb = _mb_decode(kv4, misc4, lo4, c_mb_k, c_mb_v, page_table, l)
        mixed_s = tuple(a.reshape(ms, GROUP_WIDTH).astype(BF16) for a in (s_sb, s_hg, s_cv, s_mb))
        xs = _out_project(mixed_s, w_out_b[l], xs, norm_post[l], tm_s)
        states_s[0].append(hg_new_s)
        states_s[1].append(cv_new_s)

    heads_p = lambda a: a.reshape(depth, b, t, N_HEADS, HEAD_DIM)
    heads_s = lambda a: a.reshape(depth, bd, td, N_HEADS, HEAD_DIM)
    return (xp.reshape(b, t, d), xs.reshape(bd, td, d),
            *(heads_p(a) for a in kv_p), jnp.stack(states_p[0]), jnp.stack(states_p[1]),
            *(heads_s(a) for a in kv_s), jnp.stack(states_s[0]), jnp.stack(states_s[1]))
```

```python
import functools

import jax
import jax.numpy as jnp
from jax import lax
from jax.experimental import pallas as pl
from jax.experimental.pallas import tpu as pltpu

F32 = jnp.float32
BF16 = jnp.bfloat16

HEAD_DIM = 128
N_HEADS = 4
GROUP_WIDTH = N_HEADS * HEAD_DIM
N_PARTS = 16
CONV_WIDTH = 3
MOBA_BLOCK = 256
MOBA_TOPK = 3
ROPE_THETA = 500000.0
ROPE_HALF = HEAD_DIM // 8
RMS_EPS = 1e-6
SCALE = HEAD_DIM ** -0.5
NEG_INF = float("-inf")

(P_SB_Q, P_SB_K, P_SB_V, P_SB_G, P_HG_Q, P_HG_F, P_HG_I, P_HG_G,
 P_CV_U, P_CV_B, P_CV_C, P_CV_G, P_MB_Q, P_MB_K, P_MB_V, P_MB_G) = range(N_PARTS)

STEP_PARTS = (P_SB_K, P_SB_V, P_MB_K, P_MB_V,
              P_HG_F, P_HG_Q, P_HG_I, P_MB_Q,
              P_SB_Q, P_SB_G, P_HG_G, P_CV_U, P_CV_B, P_CV_C, P_CV_G, P_MB_G)
N_KV, N_MISC, N_LO = 4, 4, 8
KV_SB_K, KV_SB_V, KV_MB_K, KV_MB_V = range(N_KV)
MISC_HG_F, MISC_HG_Q, MISC_HG_I, MISC_MB_Q = range(N_MISC)
LO_SB_Q, LO_SB_G, LO_HG_G, LO_CV_U, LO_CV_B, LO_CV_C, LO_CV_G, LO_MB_G = range(N_LO)
STEP_MB_K = STEP_PARTS.index(P_MB_K)
STEP_MB_Q = STEP_PARTS.index(P_MB_Q)

VMEM_LIMIT = 48 * 1024 * 1024


def _cparams(sem):
    return pltpu.CompilerParams(dimension_semantics=sem, vmem_limit_bytes=VMEM_LIMIT)


def _softplus(z):
    return jnp.maximum(z, 0.0) + jnp.log(1.0 + jnp.exp(-jnp.abs(z)))


def _silu(g):
    return g / (1.0 + jnp.exp(-g))


def _split2(x):
    hi = x.astype(BF16)
    lo = (x - hi.astype(F32)).astype(BF16)
    return hi, lo


def _split3(x):
    hi = x.astype(BF16)
    r = x - hi.astype(F32)
    mid = r.astype(BF16)
    lo = (r - mid.astype(F32)).astype(BF16)
    return hi, mid, lo


def _dot(a, b):
    return jnp.dot(a, b, preferred_element_type=F32)


def _dot_nt(a, b):
    return lax.dot_general(a, b, (((1,), (1,)), ((), ())), preferred_element_type=F32)


def _dot_tn(a, b):
    return lax.dot_general(a, b, (((0,), (0,)), ((), ())), preferred_element_type=F32)


def _layer_rows(p):
    return p.reshape(p.shape[0], 1, p.shape[1])


def _layer_row_spec(layer, width, col=None):
    return pl.BlockSpec((None, 1, width), lambda *idx: (layer, 0, 0 if col is None else col(*idx)))


def _head_rows(start, n):
    return pl.ds(start, n, stride=N_HEADS)


def _prenorm_kernel(x_ref, g_ref, h_ref):
    x = x_ref[...]
    ms = jnp.mean(x * x, axis=-1, keepdims=True)
    h_ref[...] = (x * lax.rsqrt(ms + RMS_EPS) * g_ref[...]).astype(h_ref.dtype)


def _prenorm(x, norm_pre, layer, tm):
    m, d = x.shape
    return pl.pallas_call(
        _prenorm_kernel,
        grid=(m // tm,),
        in_specs=[pl.BlockSpec((tm, d), lambda i: (i, 0)), _layer_row_spec(layer, d)],
        out_specs=pl.BlockSpec((tm, d), lambda i: (i, 0)),
        out_shape=jax.ShapeDtypeStruct((m, d), BF16),
        compiler_params=_cparams(("parallel",)),
        name="prenorm",
    )(x, _layer_rows(norm_pre))


def _proj_kernel(parts_ref, h_ref, w_ref, c_ref, s1_ref, s2_ref, *rest, tm, n_alias):
    kv_refs = rest[n_alias:n_alias + N_KV]
    misc_ref, lo_ref = rest[n_alias + N_KV:]
    n = pl.program_id(1)
    acc = _dot(h_ref[...], w_ref[...])

    def head(hh, rope):
        a = acc[:, hh * HEAD_DIM:(hh + 1) * HEAD_DIM]
        if rope:
            a = (a * c_ref[...]
                 + pltpu.roll(a, HEAD_DIM - ROPE_HALF, 1) * s1_ref[...]
                 + pltpu.roll(a, ROPE_HALF, 1) * s2_ref[...])
        return a

    for a in range(N_KV):
        @pl.when(n == a)
        def _(a=a):
            for hh in range(N_HEADS):
                kv_refs[a][0, _head_rows(hh, tm), :] = head(hh, a == STEP_MB_K)

    @pl.when(jnp.logical_and(jnp.logical_and(n >= N_KV, n < N_KV + N_MISC), n != STEP_MB_Q))
    def _():
        misc_ref[0] = acc

    @pl.when(n == STEP_MB_Q)
    def _():
        for hh in range(N_HEADS):
            misc_ref[0, :, hh * HEAD_DIM:(hh + 1) * HEAD_DIM] = head(hh, True)

    @pl.when(n >= N_KV + N_MISC)
    def _():
        lo_ref[0] = acc.astype(lo_ref.dtype)


def _rope_tables(pos):
    inv_freq = ROPE_THETA ** (-jnp.arange(ROPE_HALF, dtype=F32) / ROPE_HALF)
    ang = pos.astype(F32)[:, None] * inv_freq[None, :]
    cos, sin = jnp.cos(ang), jnp.sin(ang)
    n = pos.shape[0]
    rest = HEAD_DIM - 2 * ROPE_HALF
    c = jnp.concatenate([cos, cos, jnp.ones((n, rest), F32)], axis=1)
    s1 = jnp.concatenate([-sin, jnp.zeros((n, HEAD_DIM - ROPE_HALF), F32)], axis=1)
    s2 = jnp.concatenate([jnp.zeros((n, ROPE_HALF), F32), sin, jnp.zeros((n, rest), F32)], axis=1)
    return c, s1, s2


def _project(h, w_bf16, tables, tm, layer, depth, kv_prev, lo_dtype):
    m, d = h.shape
    period_blocks = tables[0].shape[0] // tm
    tab_spec = pl.BlockSpec((tm, HEAD_DIM), lambda i, n, parts: (i % period_blocks, 0))
    n_alias = len(kv_prev)
    clamp = lambda v, hi: jnp.minimum(jnp.maximum(v, 0), hi)
    kv_spec = pl.BlockSpec((1, tm * N_HEADS, HEAD_DIM), lambda i, n, parts: (layer, i, 0))
    outs = pl.pallas_call(
        functools.partial(_proj_kernel, tm=tm, n_alias=n_alias),
        grid_spec=pltpu.PrefetchScalarGridSpec(
            num_scalar_prefetch=1,
            grid=(m // tm, N_PARTS),
            in_specs=[
                pl.BlockSpec((tm, d), lambda i, n, parts: (i, 0)),
                pl.BlockSpec((None, d, GROUP_WIDTH), lambda i, n, parts: (layer, 0, parts[n])),
                tab_spec, tab_spec, tab_spec,
            ] + [pl.BlockSpec(memory_space=pl.ANY)] * n_alias,
            out_specs=[kv_spec] * N_KV + [
                pl.BlockSpec((1, tm, GROUP_WIDTH),
                             lambda i, n, parts: (clamp(n - N_KV, N_MISC - 1), i, 0)),
                pl.BlockSpec((1, tm, GROUP_WIDTH),
                             lambda i, n, parts: (clamp(n - N_KV - N_MISC, N_LO - 1), i, 0)),
            ],
        ),
        out_shape=[jax.ShapeDtypeStruct((depth, m * N_HEADS, HEAD_DIM), F32)] * N_KV + [
            jax.ShapeDtypeStruct((N_MISC, m, GROUP_WIDTH), F32),
            jax.ShapeDtypeStruct((N_LO, m, GROUP_WIDTH), lo_dtype),
        ],
        input_output_aliases={6 + a: a for a in range(n_alias)},
        compiler_params=_cparams(("parallel", "arbitrary")),
        name="in_proj",
    )(jnp.array(STEP_PARTS, jnp.int32), h, w_bf16, *tables, *kv_prev)
    return list(outs[:N_KV]), outs[N_KV], outs[N_KV + 1]


def _out_kernel(a_ref, b_ref, c_ref, d_ref, w_ref, x_ref, g_ref, *rest):
    acc = _dot(a_ref[...].astype(BF16), w_ref[0])
    acc += _dot(b_ref[...].astype(BF16), w_ref[1])
    acc += _dot(c_ref[...].astype(BF16), w_ref[2])
    acc += _dot(d_ref[...].astype(BF16), w_ref[3])
    ms = jnp.mean(acc * acc, axis=-1, keepdims=True)
    y = x_ref[...] + acc * lax.rsqrt(ms + RMS_EPS) * g_ref[...]
    if len(rest) == 1:
        rest[0][...] = y
    else:
        gn_ref, o_ref, h_ref = rest
        o_ref[...] = y
        ms = jnp.mean(y * y, axis=-1, keepdims=True)
        h_ref[...] = (y * lax.rsqrt(ms + RMS_EPS) * gn_ref[...]).astype(h_ref.dtype)


def _out_project(mixed, w_out, x, norm_post, norm_pre, layer, tm):
    m, d = x.shape
    depth = w_out.shape[0]
    last = layer == depth - 1
    mix_spec = pl.BlockSpec((tm, GROUP_WIDTH), lambda i: (i, 0))
    row_spec = pl.BlockSpec((tm, d), lambda i: (i, 0))
    outs = pl.pallas_call(
        _out_kernel,
        grid=(m // tm,),
        in_specs=[mix_spec, mix_spec, mix_spec, mix_spec,
                  pl.BlockSpec((None, 4, GROUP_WIDTH, d), lambda i: (layer, 0, 0, 0)),
                  row_spec, _layer_row_spec(layer, d)]
        + ([] if last else [_layer_row_spec(layer + 1, d)]),
        out_specs=row_spec if last else [row_spec, row_spec],
        out_shape=(jax.ShapeDtypeStruct((m, d), F32) if last else
                   [jax.ShapeDtypeStruct((m, d), F32), jax.ShapeDtypeStruct((m, d), BF16)]),
        compiler_params=_cparams(("parallel",)),
        name="out_proj",
    )(*mixed, w_out, x, _layer_rows(norm_post), *([] if last else [_layer_rows(norm_pre)]))
    return (outs, None) if last else tuple(outs)


def _strict_upper_ones(n):
    r = lax.broadcasted_iota(jnp.int32, (n, n), 0)
    c = lax.broadcasted_iota(jnp.int32, (n, n), 1)
    return jnp.where(r > c, 1.0, 0.0).astype(BF16)


def _reverse_cumsum_groups(lk, run, tri):
    gw = tri.shape[0]
    ng = lk.shape[1] // gw
    rows = lk.shape[0]
    groups = [lk[:, gi * gw:(gi + 1) * gw] for gi in range(ng)]
    hi, lo = _split2(jnp.concatenate(groups, axis=0))
    local = _dot(hi, tri) + _dot(lo, tri)
    pieces = [None] * ng
    for gi in range(ng - 1, -1, -1):
        pieces[gi] = local[gi * rows:(gi + 1) * rows] + run
        run = run + jnp.sum(groups[gi], axis=-1, keepdims=True)
    return jnp.concatenate(pieces, axis=1), run


def _sb_prompt_kernel(q_ref, k_ref, v_ref, g_ref, o_ref, kb_ref, vb_ref, *, tq, t):
    h = pl.program_id(1)
    qi = pl.program_id(2)

    @pl.when(qi == 0)
    def _():
        for n in range(t // tq):
            rows = _head_rows(n * tq * N_HEADS + h, tq)
            kb_ref[n * tq:(n + 1) * tq, :] = k_ref[0, rows, :].astype(BF16)
            vb_ref[n * tq:(n + 1) * tq, :] = v_ref[0, rows, :].astype(BF16)

    q = q_ref[0]
    tri = _strict_upper_ones(HEAD_DIM)

    def block(kb, run, diagonal):
        k0 = kb * tq
        z = _dot_nt(q, kb_ref[pl.ds(k0, tq), :]) * SCALE
        lk = -_softplus(z)
        if diagonal:
            mask = (lax.broadcasted_iota(jnp.int32, (tq, tq), 1)
                    < lax.broadcasted_iota(jnp.int32, (tq, tq), 0))
            lk = jnp.where(mask, lk, 0.0)
        between, run = _reverse_cumsum_groups(lk, run, tri)
        w = jnp.exp(z + lk + between)
        if diagonal:
            w = jnp.where(mask, w, 0.0)
        return _dot(w.astype(BF16), vb_ref[pl.ds(k0, tq), :]), run

    def tile(n_past):
        acc, run = block(n_past, jnp.zeros((tq, 1), F32), True)
        for kb in range(n_past - 1, -1, -1):
            part, run = block(kb, run, False)
            acc = acc + part
        o_ref[...] = (acc * _silu(g_ref[0].astype(F32))).astype(o_ref.dtype)

    for n_past in range(t // tq):
        pl.when(qi == n_past)(functools.partial(tile, n_past))


def _sb_prompt(kv, lo, layer, b, t, tq=512):
    tq = min(tq, t)
    nq = t // tq
    qspec = lambda part: pl.BlockSpec((1, tq, HEAD_DIM), lambda bi, h, i: (part, bi * nq + i, h))
    kvspec = pl.BlockSpec((1, t * N_HEADS, HEAD_DIM), lambda bi, h, i: (layer, bi, 0))
    return pl.pallas_call(
        functools.partial(_sb_prompt_kernel, tq=tq, t=t),
        grid=(b, N_HEADS, nq),
        in_specs=[qspec(LO_SB_Q), kvspec, kvspec, qspec(LO_SB_G)],
        out_specs=pl.BlockSpec((tq, HEAD_DIM), lambda bi, h, i: (bi * nq + i, h)),
        out_shape=jax.ShapeDtypeStruct((b * t, GROUP_WIDTH), BF16),
        scratch_shapes=[pltpu.VMEM((t, HEAD_DIM), BF16), pltpu.VMEM((t, HEAD_DIM), BF16)],
        compiler_params=_cparams(("parallel", "arbitrary", "arbitrary")),
        name="sb_prompt",
    )(lo, kv[KV_SB_K], kv[KV_SB_V], lo)


def _lower_bound_logs(lb_logits, layer):
    depth = lb_logits.shape[0]
    rows = [lb_logits[i:i + 1, :] for i in range(depth)]
    mx = functools.reduce(jnp.maximum, rows)
    es = [jnp.exp(r - mx) for r in rows]
    tot = functools.reduce(lambda a, c: a + c, es)
    ps = [e / tot for e in es]
    cs = ps[0]
    for i in range(1, layer + 1):
        cs = cs + ps[i]
    lb = jnp.maximum(cs - ps[0], 0.0)
    return jnp.log(lb), jnp.log1p(-lb)


def _log_forget(x, log_lb, log_1m_lb):
    a = log_lb
    bb = log_1m_lb - _softplus(-x)
    return jnp.maximum(a, bb) + jnp.log1p(jnp.exp(-jnp.abs(a - bb)))


def _col_bcast(row):
    n = row.shape[-1]
    return jnp.broadcast_to(row, (n, n)).T


def _head_rmsnorm(o, g_row):
    ms = jnp.mean(o * o, axis=-1, keepdims=True)
    return o * lax.rsqrt(ms + RMS_EPS) * g_row


def _hg_prompt_kernel(q_ref, f_ref, i_ref, g_ref, lb_ref, gn_ref, o_ref, s_out_ref, s_ref,
                      *, c, layer):
    ci = pl.program_id(2)
    nc = pl.num_programs(2)
    d = HEAD_DIM

    @pl.when(ci == 0)
    def _():
        s_ref[...] = jnp.zeros_like(s_ref)

    log_lb, log_1m_lb = _lower_bound_logs(lb_ref[...], layer)
    g = _log_forget(f_ref[0], log_lb, log_1m_lb)
    kk = 1.0 - jnp.exp(g)
    qs = q_ref[0] * SCALE
    v = i_ref[0]
    vb = v.astype(BF16)

    r = lax.broadcasted_iota(jnp.int32, (c, c), 0)
    cc = lax.broadcasted_iota(jnp.int32, (c, c), 1)
    lower = jnp.where(r >= cc, 1.0, 0.0).astype(BF16)
    g_hi, g_mid, g_lo = _split3(g)
    gc = _dot(lower, g_hi) + _dot(lower, g_mid) + _dot(lower, g_lo)
    g_last = gc[c - 1:c, :]

    s = s_ref[...]
    o = _dot((qs * jnp.exp(gc)).astype(BF16), s.astype(BF16))

    a = jnp.zeros((c, c), F32)
    blk = 8
    while blk < c:
        nb = c // blk
        gc3 = gc.reshape(nb, blk, d)
        g_end = gc3[:, blk - 1:blk, :]
        g_prev = jnp.concatenate([jnp.zeros((1, 1, d), F32), g_end[:nb - 1]], axis=0)
        qd = (qs.reshape(nb, blk, d) * jnp.exp(gc3 - g_prev)).reshape(c, d)
        kd = (kk.reshape(nb, blk, d) * jnp.exp(g_end - gc3)).reshape(c, d)
        sc = _dot_nt(qd.astype(BF16), kd.astype(BF16))
        rb = r // blk
        cb = cc // blk
        pair = jnp.logical_and(rb % 2 == 1, cb == rb - 1)
        a = jnp.where(pair, sc, a)
        blk *= 2
    o = o + _dot(a.astype(BF16), vb)

    n8 = c // 8
    gc8 = gc.reshape(n8, 8, d)
    q8 = qs.reshape(n8, 8, d)
    k8 = kk.reshape(n8, 8, d)
    v8 = v.reshape(n8, 8, d)
    sub = lax.broadcasted_iota(jnp.int32, (n8, 8, d), 1)
    od = jnp.zeros((n8, 8, d), F32)
    for j in range(8):
        diff = gc8[:, j:j + 1, :] - gc8
        e = jnp.exp(jnp.where(sub <= j, diff, NEG_INF))
        sc = jnp.sum(e * k8 * q8[:, j:j + 1, :], axis=-1, keepdims=True)
        oj = jnp.sum(sc * v8, axis=1, keepdims=True)
        od = jnp.where(sub == j, oj, od)
    o = o + od.reshape(c, d)

    kd = (kk * jnp.exp(g_last - gc)).astype(BF16)
    s_new = _col_bcast(jnp.exp(g_last)) * s + _dot_tn(kd, vb)
    s_ref[...] = s_new

    @pl.when(ci == nc - 1)
    def _():
        s_out_ref[0, 0] = s_new

    o_ref[...] = (_head_rmsnorm(o, gn_ref[...]) * _silu(g_ref[0].astype(F32))).astype(o_ref.dtype)


def _hg_prompt(misc, lo, lb_logits, g_hg, layer, b, t, c=256):
    c = min(c, t)
    nc = t // c
    depth = lb_logits.shape[0]
    spec = lambda part: pl.BlockSpec((1, c, HEAD_DIM), lambda bi, h, i: (part, bi * nc + i, h))
    return pl.pallas_call(
        functools.partial(_hg_prompt_kernel, c=c, layer=layer),
        grid=(b, N_HEADS, nc),
        in_specs=[spec(MISC_HG_Q), spec(MISC_HG_F), spec(MISC_HG_I), spec(LO_HG_G),
                  pl.BlockSpec((depth, HEAD_DIM), lambda bi, h, i: (0, h)),
                  _layer_row_spec(layer, HEAD_DIM, col=lambda bi, h, i: h)],
        out_specs=[pl.BlockSpec((c, HEAD_DIM), lambda bi, h, i: (bi * nc + i, h)),
                   pl.BlockSpec((1, 1, HEAD_DIM, HEAD_DIM), lambda bi, h, i: (bi, h, 0, 0))],
        out_shape=[jax.ShapeDtypeStruct((b * t, GROUP_WIDTH), BF16),
                   jax.ShapeDtypeStruct((b, N_HEADS, HEAD_DIM, HEAD_DIM), F32)],
        scratch_shapes=[pltpu.VMEM((HEAD_DIM, HEAD_DIM), F32)],
        compiler_params=_cparams(("parallel", "parallel", "arbitrary")),
        name="hgrn_prompt",
    )(misc, misc, misc, lo, lb_logits, _layer_rows(g_hg))


def _conv_prompt_kernel(u_ref, b_ref, c_ref, g_ref, w_ref, o_ref, st_ref, carry_ref, *, tt):
    ti = pl.program_id(1)
    nt = pl.num_programs(1)

    @pl.when(ti == 0)
    def _():
        carry_ref[...] = jnp.zeros_like(carry_ref)

    z = c_ref[0].astype(F32) * u_ref[0].astype(F32)
    row = lax.broadcasted_iota(jnp.int32, z.shape, 0)
    p0 = carry_ref[0:1, :]
    p1 = carry_ref[1:2, :]
    z1 = jnp.where(row == 0, p1, pltpu.roll(z, 1, 0))
    z2 = jnp.where(row == 0, p0, jnp.where(row == 1, p1, pltpu.roll(z, 2, 0)))
    y = z2 * w_ref[0:1, :] + z1 * w_ref[1:2, :] + z * w_ref[2:3, :]
    o_ref[...] = (b_ref[0].astype(F32) * y * _silu(g_ref[0].astype(F32))).astype(o_ref.dtype)
    tail = z[tt - 2:tt, :]
    carry_ref[0:2, :] = tail

    @pl.when(ti == nt - 1)
    def _():
        st_ref[0] = tail


def _conv_prompt(lo, conv_w, layer, b, t, tt=512):
    tt = min(tt, t)
    nt = t // tt
    spec = lambda part: pl.BlockSpec((1, tt, GROUP_WIDTH), lambda bi, i: (part, bi * nt + i, 0))
    return pl.pallas_call(
        functools.partial(_conv_prompt_kernel, tt=tt),
        grid=(b, nt),
        in_specs=[spec(LO_CV_U), spec(LO_CV_B), spec(LO_CV_C), spec(LO_CV_G),
                  pl.BlockSpec((None, CONV_WIDTH, GROUP_WIDTH), lambda bi, i: (layer, 0, 0))],
        out_specs=[pl.BlockSpec((tt, GROUP_WIDTH), lambda bi, i: (bi * nt + i, 0)),
                   pl.BlockSpec((1, CONV_WIDTH - 1, GROUP_WIDTH), lambda bi, i: (bi, 0, 0))],
        out_shape=[jax.ShapeDtypeStruct((b * t, GROUP_WIDTH), BF16),
                   jax.ShapeDtypeStruct((b, CONV_WIDTH - 1, GROUP_WIDTH), F32)],
        scratch_shapes=[pltpu.VMEM((8, GROUP_WIDTH), F32)],
        compiler_params=_cparams(("parallel", "arbitrary")),
        name="conv_prompt",
    )(lo, lo, lo, lo, conv_w)


def _lane_pick(mat, idx):
    lane = lax.broadcasted_iota(jnp.int32, mat.shape, 1)
    return jnp.max(jnp.where(lane == idx, mat, NEG_INF), axis=-1, keepdims=True)


def _moba_select_t(gate_t, n_past, n_blocks):
    row = lax.broadcasted_iota(jnp.int32, gate_t.shape, 0)
    gm = jnp.where(row < n_past, gate_t, NEG_INF)
    sel = jnp.full(gate_t.shape, NEG_INF, F32)
    for n in range(n_blocks):
        cur = gm[n:n + 1, :]
        beats = jnp.where(gm > cur, 1.0, jnp.where(jnp.logical_and(gm == cur, row < n), 1.0, 0.0))
        rank = jnp.sum(beats, axis=0, keepdims=True)
        ok = jnp.logical_and(rank < MOBA_TOPK, cur > NEG_INF)
        sel = jnp.where(jnp.logical_and(row == n, ok), 0.0, sel)
    return sel


def _mb_prompt_kernel(q_ref, k_ref, v_ref, g_ref, o_ref, kmean_ref, kb_ref, vt_ref, *, nb, qt):
    h = pl.program_id(1)
    qi = pl.program_id(2)
    blk = MOBA_BLOCK

    @pl.when(qi == 0)
    def _():
        kmean_ref[...] = jnp.zeros_like(kmean_ref)
        for n in range(nb):
            rows = _head_rows(n * blk * N_HEADS + h, blk)
            kn = k_ref[0, rows, :]
            kmean_ref[n:n + 1, :] = jnp.mean(kn, axis=0, keepdims=True)
            kb_ref[n * blk:(n + 1) * blk, :] = kn.astype(BF16)
            vt_ref[:, n * blk:(n + 1) * blk] = v_ref[0, rows, :].T.astype(BF16)

    q = q_ref[0]
    qb = q.astype(BF16)
    per = qt // blk
    gate_t = lax.dot_general(kmean_ref[...], q, (((1,), (1,)), ((), ())),
                             precision=lax.Precision.HIGHEST, preferred_element_type=F32)
    own = qi * per + lax.broadcasted_iota(jnp.int32, (1, qt), 1) // blk
    sel_t = _moba_select_t(gate_t, own, nb)

    rel = (lax.broadcasted_iota(jnp.int32, (blk, qt), 0)
           - lax.broadcasted_iota(jnp.int32, (blk, qt), 1))

    def attend(n_blk):
        parts = []
        for j in range(n_blk):
            s = (_dot_nt(kb_ref[j * blk:(j + 1) * blk, :], qb) * SCALE
                 + jnp.where(j < own, sel_t[j:j + 1, :], 0.0))
            parts.append(jnp.where(rel <= (qi * per - j) * blk, s, NEG_INF))
        m = functools.reduce(jnp.maximum, [jnp.max(s, axis=0, keepdims=True) for s in parts])
        ps = [jnp.exp(s - m) for s in parts]
        l = functools.reduce(lambda a, c: a + c, [jnp.sum(p, axis=0, keepdims=True) for p in ps])
        p_all = jnp.concatenate([p.astype(BF16) for p in ps], axis=0)
        acc = _dot(vt_ref[:, :n_blk * blk], p_all)
        o_ref[...] = ((acc / l).T * _silu(g_ref[0].astype(F32))).astype(o_ref.dtype)

    for tile in range(nb // per):
        pl.when(qi == tile)(functools.partial(attend, per * (tile + 1)))


def _mb_prompt(kv, misc, lo, layer, b, t, qt=512):
    blk = MOBA_BLOCK
    nb = t // blk
    nb_pad = -(-nb // 8) * 8
    qt = min(qt, t)
    nq = t // qt
    qspec = lambda part: pl.BlockSpec((1, qt, HEAD_DIM), lambda bi, h, i: (part, bi * nq + i, h))
    kvspec = pl.BlockSpec((1, t * N_HEADS, HEAD_DIM), lambda bi, h, i: (layer, bi, 0))
    return pl.pallas_call(
        functools.partial(_mb_prompt_kernel, nb=nb, qt=qt),
        grid=(b, N_HEADS, nq),
        in_specs=[qspec(MISC_MB_Q), kvspec, kvspec, qspec(LO_MB_G)],
        out_specs=pl.BlockSpec((qt, HEAD_DIM), lambda bi, h, i: (bi * nq + i, h)),
        out_shape=jax.ShapeDtypeStruct((b * t, GROUP_WIDTH), BF16),
        scratch_shapes=[pltpu.VMEM((nb_pad, HEAD_DIM), F32), pltpu.VMEM((t, HEAD_DIM), BF16),
                        pltpu.VMEM((HEAD_DIM, t), BF16)],
        compiler_params=_cparams(("parallel", "arbitrary", "arbitrary")),
        name="moba_prompt",
    )(misc, kv[KV_MB_K], kv[KV_MB_V], lo)


def _stack_heads(x):
    return jnp.concatenate([x[:, h * HEAD_DIM:(h + 1) * HEAD_DIM] for h in range(N_HEADS)], axis=0)


def _spread_row(x, j, t):
    return jnp.concatenate(
        [jnp.broadcast_to(x[j * N_HEADS + h:j * N_HEADS + h + 1, :], (t, HEAD_DIM))
         for h in range(N_HEADS)], axis=0)


def _store_heads(o_ref, acc, gate, t):
    for h in range(N_HEADS):
        sl = slice(h * HEAD_DIM, (h + 1) * HEAD_DIM)
        o_ref[0, :, sl] = acc[h * t:(h + 1) * t, :] * _silu(gate[:, sl])


def _own_head(rows, cols, t):
    r = lax.broadcasted_iota(jnp.int32, (rows, cols), 0)
    c = lax.broadcasted_iota(jnp.int32, (rows, cols), 1)
    return c % N_HEADS == r // t


def _page_specs(layer, n_steps, n_group, page_rows, reverse):
    specs = []
    for gi in range(n_group):
        if reverse:
            imap = lambda bi, s, pt, gi=gi: (layer, pt[bi, (n_steps - 1 - s) * n_group + gi], 0, 0)
        else:
            imap = lambda bi, s, pt, gi=gi: (layer, pt[bi, s * n_group + gi], 0, 0)
        specs.append(pl.BlockSpec((None, None, page_rows, HEAD_DIM), imap))
    return specs


def _tok_spec(part, t):
    return pl.BlockSpec((1, 1, t, GROUP_WIDTH), lambda bi, *_: (part, bi, 0, 0))


def _new_kv_spec(layer, t):
    return pl.BlockSpec((1, 1, t * N_HEADS, HEAD_DIM), lambda bi, *_: (layer, bi, 0, 0))


def _sb_decode_kernel(pt_ref, q_ref, kn_ref, vn_ref, g_ref, *rest, t, n_group):
    k_refs = rest[:n_group]
    v_refs = rest[n_group:2 * n_group]
    o_ref, acc_ref, run_ref, qr_ref = rest[2 * n_group:]
    s = pl.program_id(1)
    ns = pl.num_programs(1)
    pr = k_refs[0].shape[0]
    rows = N_HEADS * t

    @pl.when(s == 0)
    def _():
        qr = _stack_heads(q_ref[0, 0])
        qr_ref[...] = qr
        tq = lax.broadcasted_iota(jnp.int32, (rows, 1), 0) % t
        kn = kn_ref[0, 0]
        vn = vn_ref[0, 0]
        acc = jnp.zeros((rows, HEAD_DIM), F32)
        run = jnp.zeros((rows, 1), F32)
        for j in range(t - 1, -1, -1):
            z = jnp.sum(qr * _spread_row(kn, j, t), axis=-1, keepdims=True) * SCALE
            ok = tq > j
            lk = jnp.where(ok, -_softplus(z), 0.0)
            w = jnp.where(ok, jnp.exp(z + lk + run), 0.0)
            acc = acc + w * _spread_row(vn, j, t)
            run = run + lk
        acc_ref[...] = acc
        run_ref[...] = jnp.broadcast_to(run, run_ref.shape)

    qb = qr_ref[...].astype(BF16)
    tri = _strict_upper_ones(HEAD_DIM)
    own = _own_head(rows, n_group * pr, t)
    z = jnp.concatenate([_dot_nt(qb, k_refs[gi][...].astype(BF16)) for gi in range(n_group)],
                        axis=1) * SCALE
    lk = jnp.where(own, -_softplus(z), 0.0)
    between, run = _reverse_cumsum_groups(lk, run_ref[:, 0:1], tri)
    w = jnp.where(own, jnp.exp(z + lk + between), 0.0).astype(BF16)
    acc = acc_ref[...]
    for gi in range(n_group):
        acc = acc + _dot(w[:, gi * pr:(gi + 1) * pr], v_refs[gi][...].astype(BF16))
    acc_ref[...] = acc
    run_ref[...] = jnp.broadcast_to(run, run_ref.shape)

    @pl.when(s == ns - 1)
    def _():
        _store_heads(o_ref, acc, g_ref[0, 0], t)


def _sb_decode(kv4, lo4, cache_k, cache_v, page_table, layer, n_group=16):
    _, bd, t, _ = lo4.shape
    n_pages = page_table.shape[1]
    pr = cache_k.shape[2]
    n_group = min(n_group, n_pages)
    n_steps = n_pages // n_group
    rows = N_HEADS * t
    grid_spec = pltpu.PrefetchScalarGridSpec(
        num_scalar_prefetch=1,
        grid=(bd, n_steps),
        in_specs=[_tok_spec(LO_SB_Q, t), _new_kv_spec(layer, t), _new_kv_spec(layer, t), _tok_spec(LO_SB_G, t)]
        + _page_specs(layer, n_steps, n_group, pr, True)
        + _page_specs(layer, n_steps, n_group, pr, True),
        out_specs=pl.BlockSpec((1, t, GROUP_WIDTH), lambda bi, s, pt: (bi, 0, 0)),
        scratch_shapes=[pltpu.VMEM((rows, HEAD_DIM), F32),
                        pltpu.VMEM((rows, HEAD_DIM), F32),
                        pltpu.VMEM((rows, HEAD_DIM), F32)],
    )
    return pl.pallas_call(
        functools.partial(_sb_decode_kernel, t=t, n_group=n_group),
        grid_spec=grid_spec,
        out_shape=jax.ShapeDtypeStruct((bd, t, GROUP_WIDTH), F32),
        compiler_params=_cparams(("parallel", "arbitrary")),
        name="sb_decode",
    )(page_table, lo4, kv4[KV_SB_K], kv4[KV_SB_V], lo4, *([cache_k] * n_group), *([cache_v] * n_group))


def _mb_gate_kernel(pt_ref, q_ref, *rest, t, n_group):
    k_refs = rest[:n_group]
    gate_ref, logit_ref, qr_ref = rest[n_group:]
    s = pl.program_id(1)
    pr = k_refs[0].shape[0]
    per_blk = MOBA_BLOCK * N_HEADS // pr

    @pl.when(s == 0)
    def _():
        qr_ref[...] = _stack_heads(q_ref[0, 0])
        gate_ref[...] = jnp.full(gate_ref.shape, NEG_INF, F32)

    qr = qr_ref[...]
    qb = qr.astype(BF16)
    lane = lax.broadcasted_iota(jnp.int32, gate_ref.shape[1:], 1)
    gates = gate_ref[0]
    for bi in range(n_group // per_blk):
        ksum = jnp.zeros((8, HEAD_DIM), F32)
        for pi in range(per_blk):
            gi = bi * per_blk + pi
            k = k_refs[gi][...]
            ksum = ksum + jnp.sum(k.reshape(pr // 8, 8, HEAD_DIM), axis=0)
            logit_ref[0, :, gi * pr:(gi + 1) * pr] = _dot_nt(qb, k.astype(BF16))
        kmean = (ksum[0:N_HEADS] + ksum[N_HEADS:2 * N_HEADS]) * (1.0 / MOBA_BLOCK)
        kmean_rows = jnp.concatenate(
            [jnp.broadcast_to(kmean[h:h + 1], (t, HEAD_DIM)) for h in range(N_HEADS)], axis=0)
        col = jnp.sum(qr * kmean_rows, axis=-1, keepdims=True)
        gates = jnp.where(lane == s * (n_group // per_blk) + bi, col, gates)
    gate_ref[0] = gates


def _mb_attend_kernel(pt_ref, q_ref, kn_ref, vn_ref, g_ref, gate_ref, logit_ref, *rest,
                      t, n_group, n_past):
    v_refs = rest[:n_group]
    o_ref, acc_ref, m_ref, l_ref, sel_ref = rest[n_group:]
    s = pl.program_id(1)
    ns = pl.num_programs(1)
    pr = v_refs[0].shape[0]
    per_blk = MOBA_BLOCK * N_HEADS // pr
    rows = N_HEADS * t

    @pl.when(s == 0)
    def _():
        neg = lambda n: jnp.full((n, HEAD_DIM), NEG_INF, F32)
        nbp = -(-n_past // 8) * 8
        gate_t = jnp.concatenate([gate_ref[0], neg(HEAD_DIM - rows)], axis=0).T
        sel_t = _moba_select_t(gate_t[:nbp], n_past, n_past)
        sel_ref[...] = jnp.concatenate([sel_t, neg(HEAD_DIM - nbp)], axis=0).T[:rows]
        qr = _stack_heads(q_ref[0, 0])
        tq = lax.broadcasted_iota(jnp.int32, (rows, 1), 0) % t
        kn = kn_ref[0, 0]
        vn = vn_ref[0, 0]
        zs = []
        for j in range(t):
            z = jnp.sum(qr * _spread_row(kn, j, t), axis=-1, keepdims=True) * SCALE
            zs.append(jnp.where(tq >= j, z, NEG_INF))
        m = functools.reduce(jnp.maximum, zs)
        l = jnp.zeros((rows, 1), F32)
        acc = jnp.zeros((rows, HEAD_DIM), F32)
        for j in range(t):
            p = jnp.exp(zs[j] - m)
            l = l + p
            acc = acc + p * _spread_row(vn, j, t)
        acc_ref[...] = acc
        m_ref[...] = jnp.broadcast_to(m, m_ref.shape)
        l_ref[...] = jnp.broadcast_to(l, l_ref.shape)

    acc = acc_ref[...]
    m = m_ref[:, 0:1]
    l = l_ref[:, 0:1]
    sel = sel_ref[...]
    own = _own_head(rows, n_group * pr, t)
    n_blk = n_group // per_blk
    bias = jnp.concatenate(
        [jnp.broadcast_to(_lane_pick(sel, s * n_blk + bi), (rows, per_blk * pr)) for bi in range(n_blk)],
        axis=1)
    sc = jnp.where(own, logit_ref[0] * SCALE + bias, NEG_INF)
    m_new = jnp.maximum(m, jnp.max(sc, axis=-1, keepdims=True))
    alpha = jnp.exp(m - m_new)
    p = jnp.exp(sc - m_new)
    l = alpha * l + jnp.sum(p, axis=-1, keepdims=True)
    pb = p.astype(BF16)
    acc = alpha * acc
    for gi in range(n_group):
        acc = acc + _dot(pb[:, gi * pr:(gi + 1) * pr], v_refs[gi][...].astype(BF16))
    m = m_new
    acc_ref[...] = acc
    m_ref[...] = jnp.broadcast_to(m, m_ref.shape)
    l_ref[...] = jnp.broadcast_to(l, l_ref.shape)

    @pl.when(s == ns - 1)
    def _():
        _store_heads(o_ref, acc / l, g_ref[0, 0], t)


def _mb_decode(kv4, misc4, lo4, cache_k, cache_v, page_table, layer, n_group=16):
    _, bd, t, _ = lo4.shape
    n_pages = page_table.shape[1]
    pr = cache_k.shape[2]
    n_group = min(n_group, n_pages)
    n_steps = n_pages // n_group
    n_past = n_pages * pr // (MOBA_BLOCK * N_HEADS)
    rows = N_HEADS * t
    gate_spec = pl.BlockSpec((1, rows, HEAD_DIM), lambda bi, s, pt: (bi, 0, 0))
    logit_spec = pl.BlockSpec((1, rows, n_group * pr), lambda bi, s, pt: (bi, 0, s))

    gates, logits = pl.pallas_call(
        functools.partial(_mb_gate_kernel, t=t, n_group=n_group),
        grid_spec=pltpu.PrefetchScalarGridSpec(
            num_scalar_prefetch=1,
            grid=(bd, n_steps),
            in_specs=[_tok_spec(MISC_MB_Q, t)] + _page_specs(layer, n_steps, n_group, pr, False),
            out_specs=[gate_spec, logit_spec],
            scratch_shapes=[pltpu.VMEM((rows, HEAD_DIM), F32)],
        ),
        out_shape=[jax.ShapeDtypeStruct((bd, rows, HEAD_DIM), F32),
                   jax.ShapeDtypeStruct((bd, rows, n_pages * pr), F32)],
        compiler_params=_cparams(("parallel", "arbitrary")),
        name="moba_decode_gate",
    )(page_table, misc4, *([cache_k] * n_group))

    return pl.pallas_call(
        functools.partial(_mb_attend_kernel, t=t, n_group=n_group, n_past=n_past),
        grid_spec=pltpu.PrefetchScalarGridSpec(
            num_scalar_prefetch=1,
            grid=(bd, n_steps),
            in_specs=[_tok_spec(MISC_MB_Q, t), _new_kv_spec(layer, t), _new_kv_spec(layer, t),
                      _tok_spec(LO_MB_G, t), gate_spec, logit_spec]
            + _page_specs(layer, n_steps, n_group, pr, False),
            out_specs=pl.BlockSpec((1, t, GROUP_WIDTH), lambda bi, s, pt: (bi, 0, 0)),
            scratch_shapes=[pltpu.VMEM((rows, HEAD_DIM), F32),
                            pltpu.VMEM((rows, HEAD_DIM), F32),
                            pltpu.VMEM((rows, HEAD_DIM), F32),
                            pltpu.VMEM((rows, HEAD_DIM), F32)],
        ),
        out_shape=jax.ShapeDtypeStruct((bd, t, GROUP_WIDTH), F32),
        compiler_params=_cparams(("parallel", "arbitrary")),
        name="moba_decode_attend",
    )(page_table, misc4, kv4[KV_MB_K], kv4[KV_MB_V], lo4, gates, logits, *([cache_v] * n_group))


def _small_decode_kernel(hq_ref, hf_ref, hi_ref, hgate_ref, cu_ref, cb_ref, cc_ref, cg_ref,
                         s0_ref, cv0_ref, lb_ref, gn_ref, w_ref,
                         ohg_ref, s_out_ref, ocv_ref, cv_out_ref, *, t, layer):
    d = HEAD_DIM
    log_lb, log_1m_lb = _lower_bound_logs(lb_ref[...], layer)
    g_all = _log_forget(hf_ref[0, 0], log_lb, log_1m_lb)
    k_all = 1.0 - jnp.exp(g_all)
    q_all = hq_ref[0, 0] * SCALE
    v_all = hi_ref[0, 0]
    gate_all = hgate_ref[0, 0]
    gn = gn_ref[...]
    pad = jnp.zeros((8 - t % 8, d), F32) if t % 8 else None

    for h in range(N_HEADS):
        sl = slice(h * d, (h + 1) * d)
        g, kk, qs, v = g_all[:, sl], k_all[:, sl], q_all[:, sl], v_all[:, sl]
        s0 = s0_ref[0, h]
        gcs = []
        run = jnp.zeros((1, d), F32)
        for i in range(t):
            run = run + g[i:i + 1]
            gcs.append(run)
        qe = jnp.concatenate([qs[i:i + 1] * jnp.exp(gcs[i]) for i in range(t)], axis=0)
        if pad is not None:
            qe = jnp.concatenate([qe, pad], axis=0)
        inter = _dot(qe.astype(BF16), s0.astype(BF16))
        s_new = _col_bcast(jnp.exp(gcs[t - 1])) * s0
        for j in range(t):
            s_new = s_new + _col_bcast(kk[j:j + 1] * jnp.exp(gcs[t - 1] - gcs[j])) * v[j:j + 1]
        s_out_ref[0, h] = s_new
        for i in range(t):
            o = inter[i:i + 1]
            for j in range(i + 1):
                sc = jnp.sum(qs[i:i + 1] * kk[j:j + 1] * jnp.exp(gcs[i] - gcs[j]),
                             axis=-1, keepdims=True)
                o = o + sc * v[j:j + 1]
            o = _head_rmsnorm(o, gn[:, sl]) * _silu(gate_all[i:i + 1, sl])
            ohg_ref[0, i:i + 1, sl] = o

    z = cc_ref[0, 0] * cu_ref[0, 0]
    zz = [cv0_ref[0, j:j + 1, :] for j in range(CONV_WIDTH - 1)] + [z[i:i + 1] for i in range(t)]
    cb = cb_ref[0, 0]
    cg = cg_ref[0, 0]
    for i in range(t):
        y = zz[i] * w_ref[0:1, :]
        for j in range(1, CONV_WIDTH):
            y = y + zz[i + j] * w_ref[j:j + 1, :]
        ocv_ref[0, i:i + 1, :] = cb[i:i + 1] * y * _silu(cg[i:i + 1])
    for j in range(CONV_WIDTH - 1):
        cv_out_ref[0, j:j + 1, :] = zz[t + j]


def _small_decode(misc4, lo4, state_hgrn, state_conv, lb_logits, g_hg, conv_w, layer):
    _, bd, t, _ = lo4.shape
    depth = lb_logits.shape[0]
    d = HEAD_DIM
    return pl.pallas_call(
        functools.partial(_small_decode_kernel, t=t, layer=layer),
        grid=(bd,),
        in_specs=[_tok_spec(MISC_HG_Q, t), _tok_spec(MISC_HG_F, t), _tok_spec(MISC_HG_I, t),
                  _tok_spec(LO_HG_G, t), _tok_spec(LO_CV_U, t), _tok_spec(LO_CV_B, t),
                  _tok_spec(LO_CV_C, t), _tok_spec(LO_CV_G, t),
                  pl.BlockSpec((None, 1, N_HEADS, d, d), lambda bi: (layer, bi, 0, 0, 0)),
                  pl.BlockSpec((None, 1, CONV_WIDTH - 1, GROUP_WIDTH), lambda bi: (layer, bi, 0, 0)),
                  pl.BlockSpec((depth, GROUP_WIDTH), lambda bi: (0, 0)),
                  _layer_row_spec(layer, GROUP_WIDTH),
                  pl.BlockSpec((None, CONV_WIDTH, GROUP_WIDTH), lambda bi: (layer, 0, 0))],
        out_specs=[pl.BlockSpec((1, t, GROUP_WIDTH), lambda bi: (bi, 0, 0)),
                   pl.BlockSpec((1, N_HEADS, d, d), lambda bi: (bi, 0, 0, 0)),
                   pl.BlockSpec((1, t, GROUP_WIDTH), lambda bi: (bi, 0, 0)),
                   pl.BlockSpec((1, CONV_WIDTH - 1, GROUP_WIDTH), lambda bi: (bi, 0, 0))],
        out_shape=[jax.ShapeDtypeStruct((bd, t, GROUP_WIDTH), F32),
                   jax.ShapeDtypeStruct((bd, N_HEADS, d, d), F32),
                   jax.ShapeDtypeStruct((bd, t, GROUP_WIDTH), F32),
                   jax.ShapeDtypeStruct((bd, CONV_WIDTH - 1, GROUP_WIDTH), F32)],
        compiler_params=_cparams(("parallel",)),
        name="hgrn_conv_decode",
    )(misc4, misc4, misc4, lo4, lo4, lo4, lo4, lo4,
      state_hgrn, state_conv, lb_logits, _layer_rows(g_hg), conv_w)


def _row_tile(m, pref):
    return pref if m % pref == 0 else m


def kernel(x_prompt, x_sample, cache_sb_k, cache_sb_v, cache_moba_k, cache_moba_v, state_hgrn, state_conv,
           page_table, w_in, w_out, norm_pre, norm_post, hgrn_out_norm, conv_w, hgrn_lb_logits):
    b, t, d = x_prompt.shape
    bd, td, _ = x_sample.shape
    depth = w_in.shape[0]
    n_phys, ps = cache_sb_k.shape[1], cache_sb_k.shape[2]
    past_len = page_table.shape[1] * ps
    assert past_len % MOBA_BLOCK == 0 and td <= MOBA_BLOCK and t % MOBA_BLOCK == 0

    mp, ms = b * t, bd * td
    tm_p, tm_s = _row_tile(mp, 1024), _row_tile(ms, 1024)
    tm_o = _row_tile(mp, 512)
    tab_p = _rope_tables(jnp.arange(t, dtype=jnp.int32))
    pos_s = past_len + jnp.arange(td, dtype=jnp.int32)
    tab_s = tuple(jnp.tile(a, (bd, 1)) for a in _rope_tables(pos_s))
    assert t % tm_p == 0 or tm_p % t == 0
    if tm_p > t:
        tab_p = tuple(jnp.tile(a, (tm_p // t, 1)) for a in tab_p)

    w_steps = w_in.astype(BF16)
    w_out_b = w_out.astype(BF16).reshape(depth, 4, GROUP_WIDTH, d)
    pool = lambda c: c.reshape(depth, n_phys, ps * N_HEADS, HEAD_DIM)
    c_sb_k, c_sb_v, c_mb_k, c_mb_v = pool(cache_sb_k), pool(cache_sb_v), pool(cache_moba_k), pool(cache_moba_v)

    xp = x_prompt.reshape(mp, d)
    xs = x_sample.reshape(ms, d)
    kv_p, kv_s = [], []
    states_p = [[] for _ in range(2)]
    states_s = [[] for _ in range(2)]
    hp = _prenorm(xp, norm_pre, 0, tm_o)
    hs = _prenorm(xs, norm_pre, 0, tm_s)
    for l in range(depth):
        kv_p, misc, lo = _project(hp, w_steps, tab_p, tm_p, l, depth, kv_p, BF16)
        o_sb = _sb_prompt(kv_p, lo, l, b, t)
        o_hg, hg_new = _hg_prompt(misc, lo, hgrn_lb_logits, hgrn_out_norm, l, b, t)
        o_cv, cv_new = _conv_prompt(lo, conv_w, l, b, t)
        o_mb = _mb_prompt(kv_p, misc, lo, l, b, t)
        xp, hp = _out_project((o_sb, o_hg, o_cv, o_mb), w_out_b, xp, norm_post, norm_pre, l, tm_o)
        states_p[0].append(hg_new)
        states_p[1].append(cv_new)

        kv_s, misc_s, lo_s = _project(hs, w_steps, tab_s, tm_s, l, depth, kv_s, F32)
        kv4 = [a.reshape(depth, bd, td * N_HEADS, HEAD_DIM) for a in kv_s]
        misc4 = misc_s.reshape(N_MISC, bd, td, GROUP_WIDTH)
        lo4 = lo_s.reshape(N_LO, bd, td, GROUP_WIDTH)
        s_sb = _sb_decode(kv4, lo4, c_sb_k, c_sb_v, page_table, l)
        s_hg, hg_new_s, s_cv, cv_new_s = _small_decode(misc4, lo4, state_hgrn, state_conv, hgrn_lb_logits,
                                                       hgrn_out_norm, conv_w, l)
        s_mb = _mb_decode(kv4, misc4, lo4, c_mb_k, c_mb_v, page_table, l)
        mixed_s = tuple(a.reshape(ms, GROUP_WIDTH) for a in (s_sb, s_hg, s_cv, s_mb))
        xs, hs = _out_project(mixed_s, w_out_b, xs, norm_post, norm_pre, l, tm_s)
        states_s[0].append(hg_new_s)
        states_s[1].append(cv_new_s)

    heads_p = lambda a: a.reshape(depth, b, t, N_HEADS, HEAD_DIM)
    heads_s = lambda a: a.reshape(depth, bd, td, N_HEADS, HEAD_DIM)
    return (xp.reshape(b, t, d), xs.reshape(bd, td, d),
            *(heads_p(a) for a in kv_p), jnp.stack(states_p[0]), jnp.stack(states_p[1]),
            *(heads_s(a) for a in kv_s), jnp.stack(states_s[0]), jnp.stack(states_s[1]))
```

```python
import functools

import jax
import jax.numpy as jnp
from jax import lax
from jax.experimental import pallas as pl
from jax.experimental.pallas import tpu as pltpu

F32 = jnp.float32
BF16 = jnp.bfloat16

HEAD_DIM = 128
N_HEADS = 4
GROUP_WIDTH = N_HEADS * HEAD_DIM
N_PARTS = 16
CONV_WIDTH = 3
MOBA_BLOCK = 256
MOBA_TOPK = 3
ROPE_THETA = 500000.0
ROPE_HALF = HEAD_DIM // 8
RMS_EPS = 1e-6
SCALE = HEAD_DIM ** -0.5
NEG_INF = float("-inf")

(P_SB_Q, P_SB_K, P_SB_V, P_SB_G, P_HG_Q, P_HG_F, P_HG_I, P_HG_G,
 P_CV_U, P_CV_B, P_CV_C, P_CV_G, P_MB_Q, P_MB_K, P_MB_V, P_MB_G) = range(N_PARTS)

STEP_PARTS = (P_SB_K, P_SB_V, P_MB_K, P_MB_V,
              P_HG_F, P_HG_Q, P_HG_I, P_MB_Q,
              P_SB_Q, P_SB_G, P_HG_G, P_CV_U, P_CV_B, P_CV_C, P_CV_G, P_MB_G)
N_KV, N_MISC, N_LO = 4, 4, 8
KV_SB_K, KV_SB_V, KV_MB_K, KV_MB_V = range(N_KV)
MISC_HG_F, MISC_HG_Q, MISC_HG_I, MISC_MB_Q = range(N_MISC)
LO_SB_Q, LO_SB_G, LO_HG_G, LO_CV_U, LO_CV_B, LO_CV_C, LO_CV_G, LO_MB_G = range(N_LO)
STEP_MB_K = STEP_PARTS.index(P_MB_K)
STEP_MB_Q = STEP_PARTS.index(P_MB_Q)

VMEM_LIMIT = 48 * 1024 * 1024


def _cparams(sem):
    return pltpu.CompilerParams(dimension_semantics=sem, vmem_limit_bytes=VMEM_LIMIT)


def _softplus(z):
    return jnp.maximum(z, 0.0) + jnp.log(1.0 + jnp.exp(-jnp.abs(z)))


def _silu(g):
    return g / (1.0 + jnp.exp(-g))


def _split2(x):
    hi = x.astype(BF16)
    lo = (x - hi.astype(F32)).astype(BF16)
    return hi, lo


def _split3(x):
    hi = x.astype(BF16)
    r = x - hi.astype(F32)
    mid = r.astype(BF16)
    lo = (r - mid.astype(F32)).astype(BF16)
    return hi, mid, lo


def _dot(a, b):
    return jnp.dot(a, b, preferred_element_type=F32)


def _dot_nt(a, b):
    return lax.dot_general(a, b, (((1,), (1,)), ((), ())), preferred_element_type=F32)


def _dot_tn(a, b):
    return lax.dot_general(a, b, (((0,), (0,)), ((), ())), preferred_element_type=F32)


def _layer_rows(p):
    return p.reshape(p.shape[0], 1, p.shape[1])


def _layer_row_spec(layer, width, col=None):
    return pl.BlockSpec((None, 1, width), lambda *idx: (layer, 0, 0 if col is None else col(*idx)))


def _head_rows(start, n):
    return pl.ds(start, n, stride=N_HEADS)


def _prenorm_kernel(x_ref, g_ref, h_ref):
    x = x_ref[...]
    ms = jnp.mean(x * x, axis=-1, keepdims=True)
    h_ref[...] = (x * lax.rsqrt(ms + RMS_EPS) * g_ref[...]).astype(h_ref.dtype)


def _prenorm(x, norm_pre, layer, tm):
    m, d = x.shape
    return pl.pallas_call(
        _prenorm_kernel,
        grid=(m // tm,),
        in_specs=[pl.BlockSpec((tm, d), lambda i: (i, 0)), _layer_row_spec(layer, d)],
        out_specs=pl.BlockSpec((tm, d), lambda i: (i, 0)),
        out_shape=jax.ShapeDtypeStruct((m, d), BF16),
        compiler_params=_cparams(("parallel",)),
        name="prenorm",
    )(x, _layer_rows(norm_pre))


def _proj_kernel(parts_ref, h_ref, w_ref, c_ref, s1_ref, s2_ref, *rest, tm, n_alias):
    kv_refs = rest[n_alias:n_alias + N_KV]
    misc_ref, lo_ref = rest[n_alias + N_KV:]
    n = pl.program_id(1)

    def head(acc, hh, rope):
        a = acc[:, hh * HEAD_DIM:(hh + 1) * HEAD_DIM]
        if rope:
            a = (a * c_ref[...]
                 + pltpu.roll(a, HEAD_DIM - ROPE_HALF, 1) * s1_ref[...]
                 + pltpu.roll(a, ROPE_HALF, 1) * s2_ref[...])
        return a

    @pl.when(n < N_KV)
    def _():
        acc = _dot(h_ref[...], w_ref[...])
        for a in range(N_KV):
            @pl.when(n == a)
            def _(a=a):
                for hh in range(N_HEADS):
                    kv_refs[a][0, _head_rows(hh, tm), :] = head(acc, hh, a == STEP_MB_K)

    @pl.when(jnp.logical_and(jnp.logical_and(n >= N_KV, n < N_KV + N_MISC), n != STEP_MB_Q))
    def _():
        misc_ref[0] = _dot(h_ref[...], w_ref[...])

    @pl.when(n == STEP_MB_Q)
    def _():
        acc = _dot(h_ref[...], w_ref[...])
        for hh in range(N_HEADS):
            misc_ref[0, :, hh * HEAD_DIM:(hh + 1) * HEAD_DIM] = head(acc, hh, True)

    @pl.when(n >= N_KV + N_MISC)
    def _():
        lo_ref[0] = _dot(h_ref[...], w_ref[...]).astype(lo_ref.dtype)


def _rope_tables(pos):
    inv_freq = ROPE_THETA ** (-jnp.arange(ROPE_HALF, dtype=F32) / ROPE_HALF)
    ang = pos.astype(F32)[:, None] * inv_freq[None, :]
    cos, sin = jnp.cos(ang), jnp.sin(ang)
    n = pos.shape[0]
    rest = HEAD_DIM - 2 * ROPE_HALF
    c = jnp.concatenate([cos, cos, jnp.ones((n, rest), F32)], axis=1)
    s1 = jnp.concatenate([-sin, jnp.zeros((n, HEAD_DIM - ROPE_HALF), F32)], axis=1)
    s2 = jnp.concatenate([jnp.zeros((n, ROPE_HALF), F32), sin, jnp.zeros((n, rest), F32)], axis=1)
    return c, s1, s2


def _project(h, w_bf16, tables, tm, layer, depth, kv_prev, lo_dtype):
    m, d = h.shape
    period_blocks = tables[0].shape[0] // tm
    tab_spec = pl.BlockSpec((tm, HEAD_DIM), lambda i, n, parts: (i % period_blocks, 0))
    n_alias = len(kv_prev)
    clamp = lambda v, hi: jnp.minimum(jnp.maximum(v, 0), hi)
    kv_spec = pl.BlockSpec((1, tm * N_HEADS, HEAD_DIM), lambda i, n, parts: (layer, i, 0))
    outs = pl.pallas_call(
        functools.partial(_proj_kernel, tm=tm, n_alias=n_alias),
        grid_spec=pltpu.PrefetchScalarGridSpec(
            num_scalar_prefetch=1,
            grid=(m // tm, N_PARTS),
            in_specs=[
                pl.BlockSpec((tm, d), lambda i, n, parts: (i, 0)),
                pl.BlockSpec((None, d, GROUP_WIDTH), lambda i, n, parts: (layer, 0, parts[n])),
                tab_spec, tab_spec, tab_spec,
            ] + [pl.BlockSpec(memory_space=pl.ANY)] * n_alias,
            out_specs=[kv_spec] * N_KV + [
                pl.BlockSpec((1, tm, GROUP_WIDTH),
                             lambda i, n, parts: (clamp(n - N_KV, N_MISC - 1), i, 0)),
                pl.BlockSpec((1, tm, GROUP_WIDTH),
                             lambda i, n, parts: (clamp(n - N_KV - N_MISC, N_LO - 1), i, 0)),
            ],
        ),
        out_shape=[jax.ShapeDtypeStruct((depth, m * N_HEADS, HEAD_DIM), F32)] * N_KV + [
            jax.ShapeDtypeStruct((N_MISC, m, GROUP_WIDTH), F32),
            jax.ShapeDtypeStruct((N_LO, m, GROUP_WIDTH), lo_dtype),
        ],
        input_output_aliases={6 + a: a for a in range(n_alias)},
        compiler_params=_cparams(("parallel", "arbitrary")),
        name="in_proj",
    )(jnp.array(STEP_PARTS, jnp.int32), h, w_bf16, *tables, *kv_prev)
    return list(outs[:N_KV]), outs[N_KV], outs[N_KV + 1]


def _out_kernel(a_ref, b_ref, c_ref, d_ref, w_ref, x_ref, g_ref, *rest):
    acc = _dot(a_ref[...].astype(BF16), w_ref[0])
    acc += _dot(b_ref[...].astype(BF16), w_ref[1])
    acc += _dot(c_ref[...].astype(BF16), w_ref[2])
    acc += _dot(d_ref[...].astype(BF16), w_ref[3])
    ms = jnp.mean(acc * acc, axis=-1, keepdims=True)
    y = x_ref[...] + acc * lax.rsqrt(ms + RMS_EPS) * g_ref[...]
    if len(rest) == 1:
        rest[0][...] = y
    else:
        gn_ref, o_ref, h_ref = rest
        o_ref[...] = y
        ms = jnp.mean(y * y, axis=-1, keepdims=True)
        h_ref[...] = (y * lax.rsqrt(ms + RMS_EPS) * gn_ref[...]).astype(h_ref.dtype)


def _out_project(mixed, w_out, x, norm_post, norm_pre, layer, tm):
    m, d = x.shape
    depth = w_out.shape[0]
    last = layer == depth - 1
    mix_spec = pl.BlockSpec((tm, GROUP_WIDTH), lambda i: (i, 0))
    row_spec = pl.BlockSpec((tm, d), lambda i: (i, 0))
    outs = pl.pallas_call(
        _out_kernel,
        grid=(m // tm,),
        in_specs=[mix_spec, mix_spec, mix_spec, mix_spec,
                  pl.BlockSpec((None, 4, GROUP_WIDTH, d), lambda i: (layer, 0, 0, 0)),
                  row_spec, _layer_row_spec(layer, d)]
        + ([] if last else [_layer_row_spec(layer + 1, d)]),
        out_specs=row_spec if last else [row_spec, row_spec],
        out_shape=(jax.ShapeDtypeStruct((m, d), F32) if last else
                   [jax.ShapeDtypeStruct((m, d), F32), jax.ShapeDtypeStruct((m, d), BF16)]),
        compiler_params=_cparams(("parallel",)),
        name="out_proj",
    )(*mixed, w_out, x, _layer_rows(norm_post), *([] if last else [_layer_rows(norm_pre)]))
    return (outs, None) if last else tuple(outs)


def _strict_upper_ones(n):
    r = lax.broadcasted_iota(jnp.int32, (n, n), 0)
    c = lax.broadcasted_iota(jnp.int32, (n, n), 1)
    return jnp.where(r > c, 1.0, 0.0).astype(BF16)


def _reverse_cumsum_groups(lk, run, tri):
    gw = tri.shape[0]
    ng = lk.shape[1] // gw
    rows = lk.shape[0]
    groups = [lk[:, gi * gw:(gi + 1) * gw] for gi in range(ng)]
    hi, lo = _split2(jnp.concatenate(groups, axis=0))
    local = _dot(hi, tri) + _dot(lo, tri)
    pieces = [None] * ng
    for gi in range(ng - 1, -1, -1):
        pieces[gi] = local[gi * rows:(gi + 1) * rows] + run
        run = run + jnp.sum(groups[gi], axis=-1, keepdims=True)
    return jnp.concatenate(pieces, axis=1), run


def _sb_prompt_kernel(q_ref, k_ref, v_ref, g_ref, o_ref, kb_ref, vb_ref, *, tq, t):
    h = pl.program_id(1)
    qi = pl.program_id(2)

    @pl.when(qi == 0)
    def _():
        for n in range(t // tq):
            rows = _head_rows(n * tq * N_HEADS + h, tq)
            kb_ref[n * tq:(n + 1) * tq, :] = k_ref[0, rows, :].astype(BF16)
            vb_ref[n * tq:(n + 1) * tq, :] = v_ref[0, rows, :].astype(BF16)

    q = q_ref[0]
    tri = _strict_upper_ones(HEAD_DIM)

    def block(kb, run, diagonal):
        k0 = kb * tq
        z = _dot_nt(q, kb_ref[pl.ds(k0, tq), :]) * SCALE
        lk = -_softplus(z)
        if diagonal:
            mask = (lax.broadcasted_iota(jnp.int32, (tq, tq), 1)
                    < lax.broadcasted_iota(jnp.int32, (tq, tq), 0))
            lk = jnp.where(mask, lk, 0.0)
        between, run = _reverse_cumsum_groups(lk, run, tri)
        w = jnp.exp(z + lk + between)
        if diagonal:
            w = jnp.where(mask, w, 0.0)
        return _dot(w.astype(BF16), vb_ref[pl.ds(k0, tq), :]), run

    def tile(n_past):
        acc, run = block(n_past, jnp.zeros((tq, 1), F32), True)
        for kb in range(n_past - 1, -1, -1):
            part, run = block(kb, run, False)
            acc = acc + part
        o_ref[...] = (acc * _silu(g_ref[0].astype(F32))).astype(o_ref.dtype)

    for n_past in range(t // tq):
        pl.when(qi == n_past)(functools.partial(tile, n_past))


def _sb_prompt(kv, lo, layer, b, t, tq=512):
    tq = min(tq, t)
    nq = t // tq
    qspec = lambda part: pl.BlockSpec((1, tq, HEAD_DIM), lambda bi, h, i: (part, bi * nq + i, h))
    kvspec = pl.BlockSpec((1, t * N_HEADS, HEAD_DIM), lambda bi, h, i: (layer, bi, 0))
    return pl.pallas_call(
        functools.partial(_sb_prompt_kernel, tq=tq, t=t),
        grid=(b, N_HEADS, nq),
        in_specs=[qspec(LO_SB_Q), kvspec, kvspec, qspec(LO_SB_G)],
        out_specs=pl.BlockSpec((tq, HEAD_DIM), lambda bi, h, i: (bi * nq + i, h)),
        out_shape=jax.ShapeDtypeStruct((b * t, GROUP_WIDTH), BF16),
        scratch_shapes=[pltpu.VMEM((t, HEAD_DIM), BF16), pltpu.VMEM((t, HEAD_DIM), BF16)],
        compiler_params=_cparams(("parallel", "arbitrary", "arbitrary")),
        name="sb_prompt",
    )(lo, kv[KV_SB_K], kv[KV_SB_V], lo)


def _lower_bound_logs(lb_logits, layer):
    depth = lb_logits.shape[0]
    rows = [lb_logits[i:i + 1, :] for i in range(depth)]
    mx = functools.reduce(jnp.maximum, rows)
    es = [jnp.exp(r - mx) for r in rows]
    tot = functools.reduce(lambda a, c: a + c, es)
    ps = [e / tot for e in es]
    cs = ps[0]
    for i in range(1, layer + 1):
        cs = cs + ps[i]
    lb = jnp.maximum(cs - ps[0], 0.0)
    return jnp.log(lb), jnp.log1p(-lb)


def _log_forget(x, log_lb, log_1m_lb):
    a = log_lb
    bb = log_1m_lb - _softplus(-x)
    return jnp.maximum(a, bb) + jnp.log1p(jnp.exp(-jnp.abs(a - bb)))


def _col_bcast(row):
    n = row.shape[-1]
    return jnp.broadcast_to(row, (n, n)).T


def _head_rmsnorm(o, g_row):
    ms = jnp.mean(o * o, axis=-1, keepdims=True)
    return o * lax.rsqrt(ms + RMS_EPS) * g_row


def _hg_prompt_kernel(q_ref, f_ref, i_ref, g_ref, lb_ref, gn_ref, o_ref, s_out_ref, s_ref,
                      *, c, layer, heads):
    ci = pl.program_id(2)

    @pl.when(ci == 0)
    def _():
        s_ref[...] = jnp.zeros_like(s_ref)

    for hh in range(heads):
        _hg_prompt_head(q_ref, f_ref, i_ref, g_ref, lb_ref, gn_ref, o_ref, s_out_ref, s_ref,
                        hh, c=c, layer=layer)


def _hg_prompt_head(q_ref, f_ref, i_ref, g_ref, lb_ref, gn_ref, o_ref, s_out_ref, s_ref, hh, *, c, layer):
    ci = pl.program_id(2)
    nc = pl.num_programs(2)
    d = HEAD_DIM
    sl = slice(hh * d, (hh + 1) * d)

    log_lb, log_1m_lb = _lower_bound_logs(lb_ref[:, sl], layer)
    g = _log_forget(f_ref[0, :, sl], log_lb, log_1m_lb)
    kk = 1.0 - jnp.exp(g)
    qs = q_ref[0, :, sl] * SCALE
    v = i_ref[0, :, sl]
    vb = v.astype(BF16)

    r = lax.broadcasted_iota(jnp.int32, (c, c), 0)
    cc = lax.broadcasted_iota(jnp.int32, (c, c), 1)
    lower = jnp.where(r >= cc, 1.0, 0.0).astype(BF16)
    g_hi, g_mid, g_lo = _split3(g)
    gc = _dot(lower, g_hi) + _dot(lower, g_mid) + _dot(lower, g_lo)
    g_last = gc[c - 1:c, :]

    s = s_ref[hh]
    o = _dot((qs * jnp.exp(gc)).astype(BF16), s.astype(BF16))

    a = jnp.zeros((c, c), F32)
    blk = 8
    while blk < c:
        nb = c // blk
        gc3 = gc.reshape(nb, blk, d)
        g_end = gc3[:, blk - 1:blk, :]
        g_prev = jnp.concatenate([jnp.zeros((1, 1, d), F32), g_end[:nb - 1]], axis=0)
        qd = (qs.reshape(nb, blk, d) * jnp.exp(gc3 - g_prev)).reshape(c, d)
        kd = (kk.reshape(nb, blk, d) * jnp.exp(g_end - gc3)).reshape(c, d)
        sc = _dot_nt(qd.astype(BF16), kd.astype(BF16))
        rb = r // blk
        cb = cc // blk
        pair = jnp.logical_and(rb % 2 == 1, cb == rb - 1)
        a = jnp.where(pair, sc, a)
        blk *= 2
    o = o + _dot(a.astype(BF16), vb)

    n8 = c // 8
    gc8 = gc.reshape(n8, 8, d)
    q8 = qs.reshape(n8, 8, d)
    k8 = kk.reshape(n8, 8, d)
    v8 = v.reshape(n8, 8, d)
    sub = lax.broadcasted_iota(jnp.int32, (n8, 8, d), 1)
    od = jnp.zeros((n8, 8, d), F32)
    for j in range(8):
        diff = gc8[:, j:j + 1, :] - gc8
        e = jnp.exp(jnp.where(sub <= j, diff, NEG_INF))
        sc = jnp.sum(e * k8 * q8[:, j:j + 1, :], axis=-1, keepdims=True)
        oj = jnp.sum(sc * v8, axis=1, keepdims=True)
        od = jnp.where(sub == j, oj, od)
    o = o + od.reshape(c, d)

    kd = (kk * jnp.exp(g_last - gc)).astype(BF16)
    s_new = _col_bcast(jnp.exp(g_last)) * s + _dot_tn(kd, vb)
    s_ref[hh] = s_new

    @pl.when(ci == nc - 1)
    def _():
        s_out_ref[0, hh] = s_new

    gate = g_ref[0, :, sl].astype(F32)
    o_ref[:, sl] = (_head_rmsnorm(o, gn_ref[:, sl]) * _silu(gate)).astype(o_ref.dtype)


def _hg_prompt(misc, lo, lb_logits, g_hg, layer, b, t, c=256, heads=2):
    c = min(c, t)
    nc = t // c
    depth = lb_logits.shape[0]
    hw = heads * HEAD_DIM
    spec = lambda part: pl.BlockSpec((1, c, hw), lambda bi, h, i: (part, bi * nc + i, h))
    return pl.pallas_call(
        functools.partial(_hg_prompt_kernel, c=c, layer=layer, heads=heads),
        grid=(b, N_HEADS // heads, nc),
        in_specs=[spec(MISC_HG_Q), spec(MISC_HG_F), spec(MISC_HG_I), spec(LO_HG_G),
                  pl.BlockSpec((depth, hw), lambda bi, h, i: (0, h)),
                  _layer_row_spec(layer, hw, col=lambda bi, h, i: h)],
        out_specs=[pl.BlockSpec((c, hw), lambda bi, h, i: (bi * nc + i, h)),
                   pl.BlockSpec((1, heads, HEAD_DIM, HEAD_DIM), lambda bi, h, i: (bi, h, 0, 0))],
        out_shape=[jax.ShapeDtypeStruct((b * t, GROUP_WIDTH), BF16),
                   jax.ShapeDtypeStruct((b, N_HEADS, HEAD_DIM, HEAD_DIM), F32)],
        scratch_shapes=[pltpu.VMEM((heads, HEAD_DIM, HEAD_DIM), F32)],
        compiler_params=_cparams(("parallel", "parallel", "arbitrary")),
        name="hgrn_prompt",
    )(misc, misc, misc, lo, lb_logits, _layer_rows(g_hg))


def _conv_prompt_kernel(u_ref, b_ref, c_ref, g_ref, w_ref, o_ref, st_ref, carry_ref, *, tt):
    ti = pl.program_id(1)
    nt = pl.num_programs(1)

    @pl.when(ti == 0)
    def _():
        carry_ref[...] = jnp.zeros_like(carry_ref)

    z = c_ref[0].astype(F32) * u_ref[0].astype(F32)
    row = lax.broadcasted_iota(jnp.int32, z.shape, 0)
    p0 = carry_ref[0:1, :]
    p1 = carry_ref[1:2, :]
    z1 = jnp.where(row == 0, p1, pltpu.roll(z, 1, 0))
    z2 = jnp.where(row == 0, p0, jnp.where(row == 1, p1, pltpu.roll(z, 2, 0)))
    y = z2 * w_ref[0:1, :] + z1 * w_ref[1:2, :] + z * w_ref[2:3, :]
    o_ref[...] = (b_ref[0].astype(F32) * y * _silu(g_ref[0].astype(F32))).astype(o_ref.dtype)
    tail = z[tt - 2:tt, :]
    carry_ref[0:2, :] = tail

    @pl.when(ti == nt - 1)
    def _():
        st_ref[0] = tail


def _conv_prompt(lo, conv_w, layer, b, t, tt=512):
    tt = min(tt, t)
    nt = t // tt
    spec = lambda part: pl.BlockSpec((1, tt, GROUP_WIDTH), lambda bi, i: (part, bi * nt + i, 0))
    return pl.pallas_call(
        functools.partial(_conv_prompt_kernel, tt=tt),
        grid=(b, nt),
        in_specs=[spec(LO_CV_U), spec(LO_CV_B), spec(LO_CV_C), spec(LO_CV_G),
                  pl.BlockSpec((None, CONV_WIDTH, GROUP_WIDTH), lambda bi, i: (layer, 0, 0))],
        out_specs=[pl.BlockSpec((tt, GROUP_WIDTH), lambda bi, i: (bi * nt + i, 0)),
                   pl.BlockSpec((1, CONV_WIDTH - 1, GROUP_WIDTH), lambda bi, i: (bi, 0, 0))],
        out_shape=[jax.ShapeDtypeStruct((b * t, GROUP_WIDTH), BF16),
                   jax.ShapeDtypeStruct((b, CONV_WIDTH - 1, GROUP_WIDTH), F32)],
        scratch_shapes=[pltpu.VMEM((8, GROUP_WIDTH), F32)],
        compiler_params=_cparams(("parallel", "arbitrary")),
        name="conv_prompt",
    )(lo, lo, lo, lo, conv_w)


def _lane_pick(mat, idx):
    lane = lax.broadcasted_iota(jnp.int32, mat.shape, 1)
    return jnp.max(jnp.where(lane == idx, mat, NEG_INF), axis=-1, keepdims=True)


def _moba_select_t(gate_t, n_past, n_blocks):
    row = lax.broadcasted_iota(jnp.int32, gate_t.shape, 0)
    gm = jnp.where(row < n_past, gate_t, NEG_INF)
    sel = jnp.full(gate_t.shape, NEG_INF, F32)
    for n in range(n_blocks):
        cur = gm[n:n + 1, :]
        beats = jnp.where(gm > cur, 1.0, jnp.where(jnp.logical_and(gm == cur, row < n), 1.0, 0.0))
        rank = jnp.sum(beats, axis=0, keepdims=True)
        ok = jnp.logical_and(rank < MOBA_TOPK, cur > NEG_INF)
        sel = jnp.where(jnp.logical_and(row == n, ok), 0.0, sel)
    return sel


def _mb_prompt_kernel(q_ref, k_ref, v_ref, g_ref, o_ref, kmean_ref, kb_ref, vt_ref, *, nb, qt):
    h = pl.program_id(1)
    qi = pl.program_id(2)
    blk = MOBA_BLOCK

    @pl.when(qi == 0)
    def _():
        kmean_ref[...] = jnp.zeros_like(kmean_ref)
        for n in range(nb):
            rows = _head_rows(n * blk * N_HEADS + h, blk)
            kn = k_ref[0, rows, :]
            kmean_ref[n:n + 1, :] = jnp.mean(kn, axis=0, keepdims=True)
            kb_ref[n * blk:(n + 1) * blk, :] = kn.astype(BF16)
            vt_ref[:, n * blk:(n + 1) * blk] = v_ref[0, rows, :].T.astype(BF16)

    q = q_ref[0]
    qb = q.astype(BF16)
    per = qt // blk
    gate_t = lax.dot_general(kmean_ref[...], q, (((1,), (1,)), ((), ())),
                             precision=lax.Precision.HIGHEST, preferred_element_type=F32)
    own = qi * per + lax.broadcasted_iota(jnp.int32, (1, qt), 1) // blk
    sel_t = _moba_select_t(gate_t, own, nb)

    rel = (lax.broadcasted_iota(jnp.int32, (blk, qt), 0)
           - lax.broadcasted_iota(jnp.int32, (blk, qt), 1))

    def attend(n_blk):
        parts = []
        for j in range(n_blk):
            s = (_dot_nt(kb_ref[j * blk:(j + 1) * blk, :], qb) * SCALE
                 + jnp.where(j < own, sel_t[j:j + 1, :], 0.0))
            parts.append(jnp.where(rel <= (qi * per - j) * blk, s, NEG_INF))
        m = functools.reduce(jnp.maximum, [jnp.max(s, axis=0, keepdims=True) for s in parts])
        ps = [jnp.exp(s - m) for s in parts]
        l = functools.reduce(lambda a, c: a + c, [jnp.sum(p, axis=0, keepdims=True) for p in ps])
        p_all = jnp.concatenate([p.astype(BF16) for p in ps], axis=0)
        acc = _dot(vt_ref[:, :n_blk * blk], p_all)
        o_ref[...] = ((acc / l).T * _silu(g_ref[0].astype(F32))).astype(o_ref.dtype)

    for tile in range(nb // per):
        pl.when(qi == tile)(functools.partial(attend, per * (tile + 1)))


def _mb_prompt(kv, misc, lo, layer, b, t, qt=512):
    blk = MOBA_BLOCK
    nb = t // blk
    nb_pad = -(-nb // 8) * 8
    qt = min(qt, t)
    nq = t // qt
    qspec = lambda part: pl.BlockSpec((1, qt, HEAD_DIM), lambda bi, h, i: (part, bi * nq + i, h))
    kvspec = pl.BlockSpec((1, t * N_HEADS, HEAD_DIM), lambda bi, h, i: (layer, bi, 0))
    return pl.pallas_call(
        functools.partial(_mb_prompt_kernel, nb=nb, qt=qt),
        grid=(b, N_HEADS, nq),
        in_specs=[qspec(MISC_MB_Q), kvspec, kvspec, qspec(LO_MB_G)],
        out_specs=pl.BlockSpec((qt, HEAD_DIM), lambda bi, h, i: (bi * nq + i, h)),
        out_shape=jax.ShapeDtypeStruct((b * t, GROUP_WIDTH), BF16),
        scratch_shapes=[pltpu.VMEM((nb_pad, HEAD_DIM), F32), pltpu.VMEM((t, HEAD_DIM), BF16),
                        pltpu.VMEM((HEAD_DIM, t), BF16)],
        compiler_params=_cparams(("parallel", "arbitrary", "arbitrary")),
        name="moba_prompt",
    )(misc, kv[KV_MB_K], kv[KV_MB_V], lo)


def _stack_heads(x):
    return jnp.concatenate([x[:, h * HEAD_DIM:(h + 1) * HEAD_DIM] for h in range(N_HEADS)], axis=0)


def _spread_row(x, j, t):
    return jnp.concatenate(
        [jnp.broadcast_to(x[j * N_HEADS + h:j * N_HEADS + h + 1, :], (t, HEAD_DIM))
         for h in range(N_HEADS)], axis=0)


def _store_heads(o_ref, acc, gate, t):
    for h in range(N_HEADS):
        sl = slice(h * HEAD_DIM, (h + 1) * HEAD_DIM)
        o_ref[0, :, sl] = acc[h * t:(h + 1) * t, :] * _silu(gate[:, sl])


def _own_head(rows, cols, t):
    r = lax.broadcasted_iota(jnp.int32, (rows, cols), 0)
    c = lax.broadcasted_iota(jnp.int32, (rows, cols), 1)
    return c % N_HEADS == r // t


def _page_specs(layer, n_steps, n_group, page_rows, reverse):
    specs = []
    for gi in range(n_group):
        if reverse:
            imap = lambda bi, s, pt, gi=gi: (layer, pt[bi, (n_steps - 1 - s) * n_group + gi], 0, 0)
        else:
            imap = lambda bi, s, pt, gi=gi: (layer, pt[bi, s * n_group + gi], 0, 0)
        specs.append(pl.BlockSpec((None, None, page_rows, HEAD_DIM), imap))
    return specs


def _tok_spec(part, t):
    return pl.BlockSpec((1, 1, t, GROUP_WIDTH), lambda bi, *_: (part, bi, 0, 0))


def _new_kv_spec(layer, t):
    return pl.BlockSpec((1, 1, t * N_HEADS, HEAD_DIM), lambda bi, *_: (layer, bi, 0, 0))


def _sb_decode_kernel(pt_ref, q_ref, kn_ref, vn_ref, g_ref, *rest, t, n_group):
    k_refs = rest[:n_group]
    v_refs = rest[n_group:2 * n_group]
    o_ref, acc_ref, run_ref, qr_ref = rest[2 * n_group:]
    s = pl.program_id(1)
    ns = pl.num_programs(1)
    pr = k_refs[0].shape[0]
    rows = N_HEADS * t

    @pl.when(s == 0)
    def _():
        qr = _stack_heads(q_ref[0, 0])
        qr_ref[...] = qr
        tq = lax.broadcasted_iota(jnp.int32, (rows, 1), 0) % t
        kn = kn_ref[0, 0]
        vn = vn_ref[0, 0]
        acc = jnp.zeros((rows, HEAD_DIM), F32)
        run = jnp.zeros((rows, 1), F32)
        for j in range(t - 1, -1, -1):
            z = jnp.sum(qr * _spread_row(kn, j, t), axis=-1, keepdims=True) * SCALE
            ok = tq > j
            lk = jnp.where(ok, -_softplus(z), 0.0)
            w = jnp.where(ok, jnp.exp(z + lk + run), 0.0)
            acc = acc + w * _spread_row(vn, j, t)
            run = run + lk
        acc_ref[...] = acc
        run_ref[...] = jnp.broadcast_to(run, run_ref.shape)

    qb = qr_ref[...].astype(BF16)
    tri = _strict_upper_ones(HEAD_DIM)
    own = _own_head(rows, n_group * pr, t)
    z = jnp.concatenate([_dot_nt(qb, k_refs[gi][...].astype(BF16)) for gi in range(n_group)],
                        axis=1) * SCALE
    lk = jnp.where(own, -_softplus(z), 0.0)
    between, run = _reverse_cumsum_groups(lk, run_ref[:, 0:1], tri)
    w = jnp.where(own, jnp.exp(z + lk + between), 0.0).astype(BF16)
    acc = acc_ref[...]
    for gi in range(n_group):
        acc = acc + _dot(w[:, gi * pr:(gi + 1) * pr], v_refs[gi][...].astype(BF16))
    acc_ref[...] = acc
    run_ref[...] = jnp.broadcast_to(run, run_ref.shape)

    @pl.when(s == ns - 1)
    def _():
        _store_heads(o_ref, acc, g_ref[0, 0], t)


def _sb_decode(kv4, lo4, cache_k, cache_v, page_table, layer, n_group=16):
    _, bd, t, _ = lo4.shape
    n_pages = page_table.shape[1]
    pr = cache_k.shape[2]
    n_group = min(n_group, n_pages)
    n_steps = n_pages // n_group
    rows = N_HEADS * t
    grid_spec = pltpu.PrefetchScalarGridSpec(
        num_scalar_prefetch=1,
        grid=(bd, n_steps),
        in_specs=[_tok_spec(LO_SB_Q, t), _new_kv_spec(layer, t), _new_kv_spec(layer, t), _tok_spec(LO_SB_G, t)]
        + _page_specs(layer, n_steps, n_group, pr, True)
        + _page_specs(layer, n_steps, n_group, pr, True),
        out_specs=pl.BlockSpec((1, t, GROUP_WIDTH), lambda bi, s, pt: (bi, 0, 0)),
        scratch_shapes=[pltpu.VMEM((rows, HEAD_DIM), F32),
                        pltpu.VMEM((rows, HEAD_DIM), F32),
                        pltpu.VMEM((rows, HEAD_DIM), F32)],
    )
    return pl.pallas_call(
        functools.partial(_sb_decode_kernel, t=t, n_group=n_group),
        grid_spec=grid_spec,
        out_shape=jax.ShapeDtypeStruct((bd, t, GROUP_WIDTH), F32),
        compiler_params=_cparams(("parallel", "arbitrary")),
        name="sb_decode",
    )(page_table, lo4, kv4[KV_SB_K], kv4[KV_SB_V], lo4, *([cache_k] * n_group), *([cache_v] * n_group))


def _mb_gate_kernel(pt_ref, q_ref, *rest, t, n_group):
    k_refs = rest[:n_group]
    gate_ref, logit_ref, qr_ref = rest[n_group:]
    s = pl.program_id(1)
    pr = k_refs[0].shape[0]
    per_blk = MOBA_BLOCK * N_HEADS // pr

    @pl.when(s == 0)
    def _():
        qr_ref[...] = _stack_heads(q_ref[0, 0])
        gate_ref[...] = jnp.full(gate_ref.shape, NEG_INF, F32)

    qr = qr_ref[...]
    qb = qr.astype(BF16)
    lane = lax.broadcasted_iota(jnp.int32, gate_ref.shape[1:], 1)
    gates = gate_ref[0]
    for bi in range(n_group // per_blk):
        ksum = jnp.zeros((8, HEAD_DIM), F32)
        for pi in range(per_blk):
            gi = bi * per_blk + pi
            k = k_refs[gi][...]
            ksum = ksum + jnp.sum(k.reshape(pr // 8, 8, HEAD_DIM), axis=0)
            logit_ref[0, :, gi * pr:(gi + 1) * pr] = _dot_nt(qb, k.astype(BF16))
        kmean = (ksum[0:N_HEADS] + ksum[N_HEADS:2 * N_HEADS]) * (1.0 / MOBA_BLOCK)
        kmean_rows = jnp.concatenate(
            [jnp.broadcast_to(kmean[h:h + 1], (t, HEAD_DIM)) for h in range(N_HEADS)], axis=0)
        col = jnp.sum(qr * kmean_rows, axis=-1, keepdims=True)
        gates = jnp.where(lane == s * (n_group // per_blk) + bi, col, gates)
    gate_ref[0] = gates


def _mb_attend_kernel(pt_ref, q_ref, kn_ref, vn_ref, g_ref, gate_ref, logit_ref, *rest,
                      t, n_group, n_past):
    v_refs = rest[:n_group]
    o_ref, acc_ref, m_ref, l_ref, sel_ref = rest[n_group:]
    s = pl.program_id(1)
    ns = pl.num_programs(1)
    pr = v_refs[0].shape[0]
    per_blk = MOBA_BLOCK * N_HEADS // pr
    rows = N_HEADS * t

    @pl.when(s == 0)
    def _():
        neg = lambda n: jnp.full((n, HEAD_DIM), NEG_INF, F32)
        nbp = -(-n_past // 8) * 8
        gate_t = jnp.concatenate([gate_ref[0], neg(HEAD_DIM - rows)], axis=0).T
        sel_t = _moba_select_t(gate_t[:nbp], n_past, n_past)
        sel_ref[...] = jnp.concatenate([sel_t, neg(HEAD_DIM - nbp)], axis=0).T[:rows]
        qr = _stack_heads(q_ref[0, 0])
        tq = lax.broadcasted_iota(jnp.int32, (rows, 1), 0) % t
        kn = kn_ref[0, 0]
        vn = vn_ref[0, 0]
        zs = []
        for j in range(t):
            z = jnp.sum(qr * _spread_row(kn, j, t), axis=-1, keepdims=True) * SCALE
            zs.append(jnp.where(tq >= j, z, NEG_INF))
        m = functools.reduce(jnp.maximum, zs)
        l = jnp.zeros((rows, 1), F32)
        acc = jnp.zeros((rows, HEAD_DIM), F32)
        for j in range(t):
            p = jnp.exp(zs[j] - m)
            l = l + p
            acc = acc + p * _spread_row(vn, j, t)
        acc_ref[...] = acc
        m_ref[...] = jnp.broadcast_to(m, m_ref.shape)
        l_ref[...] = jnp.broadcast_to(l, l_ref.shape)

    acc = acc_ref[...]
    m = m_ref[:, 0:1]
    l = l_ref[:, 0:1]
    sel = sel_ref[...]
    own = _own_head(rows, n_group * pr, t)
    n_blk = n_group // per_blk
    bias = jnp.concatenate(
        [jnp.broadcast_to(_lane_pick(sel, s * n_blk + bi), (rows, per_blk * pr)) for bi in range(n_blk)],
        axis=1)
    sc = jnp.where(own, logit_ref[0] * SCALE + bias, NEG_INF)
    m_new = jnp.maximum(m, jnp.max(sc, axis=-1, keepdims=True))
    alpha = jnp.exp(m - m_new)
    p = jnp.exp(sc - m_new)
    l = alpha * l + jnp.sum(p, axis=-1, keepdims=True)
    pb = p.astype(BF16)
    acc = alpha * acc
    for gi in range(n_group):
        acc = acc + _dot(pb[:, gi * pr:(gi + 1) * pr], v_refs[gi][...].astype(BF16))
    m = m_new
    acc_ref[...] = acc
    m_ref[...] = jnp.broadcast_to(m, m_ref.shape)
    l_ref[...] = jnp.broadcast_to(l, l_ref.shape)

    @pl.when(s == ns - 1)
    def _():
        _store_heads(o_ref, acc / l, g_ref[0, 0], t)


def _mb_decode(kv4, misc4, lo4, cache_k, cache_v, page_table, layer, n_group=16):
    _, bd, t, _ = lo4.shape
    n_pages = page_table.shape[1]
    pr = cache_k.shape[2]
    n_group = min(n_group, n_pages)
    n_steps = n_pages // n_group
    n_past = n_pages * pr // (MOBA_BLOCK * N_HEADS)
    rows = N_HEADS * t
    gate_spec = pl.BlockSpec((1, rows, HEAD_DIM), lambda bi, s, pt: (bi, 0, 0))
    logit_spec = pl.BlockSpec((1, rows, n_group * pr), lambda bi, s, pt: (bi, 0, s))

    gates, logits = pl.pallas_call(
        functools.partial(_mb_gate_kernel, t=t, n_group=n_group),
        grid_spec=pltpu.PrefetchScalarGridSpec(
            num_scalar_prefetch=1,
            grid=(bd, n_steps),
            in_specs=[_tok_spec(MISC_MB_Q, t)] + _page_specs(layer, n_steps, n_group, pr, False),
            out_specs=[gate_spec, logit_spec],
            scratch_shapes=[pltpu.VMEM((rows, HEAD_DIM), F32)],
        ),
        out_shape=[jax.ShapeDtypeStruct((bd, rows, HEAD_DIM), F32),
                   jax.ShapeDtypeStruct((bd, rows, n_pages * pr), F32)],
        compiler_params=_cparams(("parallel", "arbitrary")),
        name="moba_decode_gate",
    )(page_table, misc4, *([cache_k] * n_group))

    return pl.pallas_call(
        functools.partial(_mb_attend_kernel, t=t, n_group=n_group, n_past=n_past),
        grid_spec=pltpu.PrefetchScalarGridSpec(
            num_scalar_prefetch=1,
            grid=(bd, n_steps),
            in_specs=[_tok_spec(MISC_MB_Q, t), _new_kv_spec(layer, t), _new_kv_spec(layer, t),
                      _tok_spec(LO_MB_G, t), gate_spec, logit_spec]
            + _page_specs(layer, n_steps, n_group, pr, False),
            out_specs=pl.BlockSpec((1, t, GROUP_WIDTH), lambda bi, s, pt: (bi, 0, 0)),
            scratch_shapes=[pltpu.VMEM((rows, HEAD_DIM), F32),
                            pltpu.VMEM((rows, HEAD_DIM), F32),
                            pltpu.VMEM((rows, HEAD_DIM), F32),
                            pltpu.VMEM((rows, HEAD_DIM), F32)],
        ),
        out_shape=jax.ShapeDtypeStruct((bd, t, GROUP_WIDTH), F32),
        compiler_params=_cparams(("parallel", "arbitrary")),
        name="moba_decode_attend",
    )(page_table, misc4, kv4[KV_MB_K], kv4[KV_MB_V], lo4, gates, logits, *([cache_v] * n_group))


def _small_decode_kernel(hq_ref, hf_ref, hi_ref, hgate_ref, cu_ref, cb_ref, cc_ref, cg_ref,
                         s0_ref, cv0_ref, lb_ref, gn_ref, w_ref,
                         ohg_ref, s_out_ref, ocv_ref, cv_out_ref, *, t, layer):
    d = HEAD_DIM
    log_lb, log_1m_lb = _lower_bound_logs(lb_ref[...], layer)
    g_all = _log_forget(hf_ref[0, 0], log_lb, log_1m_lb)
    k_all = 1.0 - jnp.exp(g_all)
    q_all = hq_ref[0, 0] * SCALE
    v_all = hi_ref[0, 0]
    gate_all = hgate_ref[0, 0]
    gn = gn_ref[...]
    pad = jnp.zeros((8 - t % 8, d), F32) if t % 8 else None

    for h in range(N_HEADS):
        sl = slice(h * d, (h + 1) * d)
        g, kk, qs, v = g_all[:, sl], k_all[:, sl], q_all[:, sl], v_all[:, sl]
        s0 = s0_ref[0, h]
        gcs = []
        run = jnp.zeros((1, d), F32)
        for i in range(t):
            run = run + g[i:i + 1]
            gcs.append(run)
        qe = jnp.concatenate([qs[i:i + 1] * jnp.exp(gcs[i]) for i in range(t)], axis=0)
        if pad is not None:
            qe = jnp.concatenate([qe, pad], axis=0)
        inter = _dot(qe.astype(BF16), s0.astype(BF16))
        s_new = _col_bcast(jnp.exp(gcs[t - 1])) * s0
        for j in range(t):
            s_new = s_new + _col_bcast(kk[j:j + 1] * jnp.exp(gcs[t - 1] - gcs[j])) * v[j:j + 1]
        s_out_ref[0, h] = s_new
        for i in range(t):
            o = inter[i:i + 1]
            for j in range(i + 1):
                sc = jnp.sum(qs[i:i + 1] * kk[j:j + 1] * jnp.exp(gcs[i] - gcs[j]),
                             axis=-1, keepdims=True)
                o = o + sc * v[j:j + 1]
            o = _head_rmsnorm(o, gn[:, sl]) * _silu(gate_all[i:i + 1, sl])
            ohg_ref[0, i:i + 1, sl] = o

    z = cc_ref[0, 0] * cu_ref[0, 0]
    zz = [cv0_ref[0, j:j + 1, :] for j in range(CONV_WIDTH - 1)] + [z[i:i + 1] for i in range(t)]
    cb = cb_ref[0, 0]
    cg = cg_ref[0, 0]
    for i in range(t):
        y = zz[i] * w_ref[0:1, :]
        for j in range(1, CONV_WIDTH):
            y = y + zz[i + j] * w_ref[j:j + 1, :]
        ocv_ref[0, i:i + 1, :] = cb[i:i + 1] * y * _silu(cg[i:i + 1])
    for j in range(CONV_WIDTH - 1):
        cv_out_ref[0, j:j + 1, :] = zz[t + j]


def _small_decode(misc4, lo4, state_hgrn, state_conv, lb_logits, g_hg, conv_w, layer):
    _, bd, t, _ = lo4.shape
    depth = lb_logits.shape[0]
    d = HEAD_DIM
    return pl.pallas_call(
        functools.partial(_small_decode_kernel, t=t, layer=layer),
        grid=(bd,),
        in_specs=[_tok_spec(MISC_HG_Q, t), _tok_spec(MISC_HG_F, t), _tok_spec(MISC_HG_I, t),
                  _tok_spec(LO_HG_G, t), _tok_spec(LO_CV_U, t), _tok_spec(LO_CV_B, t),
                  _tok_spec(LO_CV_C, t), _tok_spec(LO_CV_G, t),
                  pl.BlockSpec((None, 1, N_HEADS, d, d), lambda bi: (layer, bi, 0, 0, 0)),
                  pl.BlockSpec((None, 1, CONV_WIDTH - 1, GROUP_WIDTH), lambda bi: (layer, bi, 0, 0)),
                  pl.BlockSpec((depth, GROUP_WIDTH), lambda bi: (0, 0)),
                  _layer_row_spec(layer, GROUP_WIDTH),
                  pl.BlockSpec((None, CONV_WIDTH, GROUP_WIDTH), lambda bi: (layer, 0, 0))],
        out_specs=[pl.BlockSpec((1, t, GROUP_WIDTH), lambda bi: (bi, 0, 0)),
                   pl.BlockSpec((1, N_HEADS, d, d), lambda bi: (bi, 0, 0, 0)),
                   pl.BlockSpec((1, t, GROUP_WIDTH), lambda bi: (bi, 0, 0)),
                   pl.BlockSpec((1, CONV_WIDTH - 1, GROUP_WIDTH), lambda bi: (bi, 0, 0))],
        out_shape=[jax.ShapeDtypeStruct((bd, t, GROUP_WIDTH), F32),
                   jax.ShapeDtypeStruct((bd, N_HEADS, d, d), F32),
                   jax.ShapeDtypeStruct((bd, t, GROUP_WIDTH), F32),
                   jax.ShapeDtypeStruct((bd, CONV_WIDTH - 1, GROUP_WIDTH), F32)],
        compiler_params=_cparams(("parallel",)),
        name="hgrn_conv_decode",
    )(misc4, misc4, misc4, lo4, lo4, lo4, lo4, lo4,
      state_hgrn, state_conv, lb_logits, _layer_rows(g_hg), conv_w)


def _row_tile(m, pref):
    return pref if m % pref == 0 else m


def kernel(x_prompt, x_sample, cache_sb_k, cache_sb_v, cache_moba_k, cache_moba_v, state_hgrn, state_conv,
           page_table, w_in, w_out, norm_pre, norm_post, hgrn_out_norm, conv_w, hgrn_lb_logits):
    b, t, d = x_prompt.shape
    bd, td, _ = x_sample.shape
    depth = w_in.shape[0]
    n_phys, ps = cache_sb_k.shape[1], cache_sb_k.shape[2]
    past_len = page_table.shape[1] * ps
    assert past_len % MOBA_BLOCK == 0 and td <= MOBA_BLOCK and t % MOBA_BLOCK == 0

    mp, ms = b * t, bd * td
    tm_p, tm_s = _row_tile(mp, 1024), _row_tile(ms, 1024)
    tm_o = _row_tile(mp, 512)
    tab_p = _rope_tables(jnp.arange(t, dtype=jnp.int32))
    pos_s = past_len + jnp.arange(td, dtype=jnp.int32)
    tab_s = tuple(jnp.tile(a, (bd, 1)) for a in _rope_tables(pos_s))
    assert t % tm_p == 0 or tm_p % t == 0
    if tm_p > t:
        tab_p = tuple(jnp.tile(a, (tm_p // t, 1)) for a in tab_p)

    w_steps = w_in.astype(BF16)
    w_out_b = w_out.astype(BF16).reshape(depth, 4, GROUP_WIDTH, d)
    pool = lambda c: c.reshape(depth, n_phys, ps * N_HEADS, HEAD_DIM)
    c_sb_k, c_sb_v, c_mb_k, c_mb_v = pool(cache_sb_k), pool(cache_sb_v), pool(cache_moba_k), pool(cache_moba_v)

    xp = x_prompt.reshape(mp, d)
    xs = x_sample.reshape(ms, d)
    kv_p, kv_s = [], []
    states_p = [[] for _ in range(2)]
    states_s = [[] for _ in range(2)]
    hp = _prenorm(xp, norm_pre, 0, tm_o)
    hs = _prenorm(xs, norm_pre, 0, tm_s)
    for l in range(depth):
        kv_p, misc, lo = _project(hp, w_steps, tab_p, tm_p, l, depth, kv_p, BF16)
        o_sb = _sb_prompt(kv_p, lo, l, b, t)
        o_hg, hg_new = _hg_prompt(misc, lo, hgrn_lb_logits, hgrn_out_norm, l, b, t)
        o_cv, cv_new = _conv_prompt(lo, conv_w, l, b, t)
        o_mb = _mb_prompt(kv_p, misc, lo, l, b, t)
        xp, hp = _out_project((o_sb, o_hg, o_cv, o_mb), w_out_b, xp, norm_post, norm_pre, l, tm_o)
        states_p[0].append(hg_new)
        states_p[1].append(cv_new)

        kv_s, misc_s, lo_s = _project(hs, w_steps, tab_s, tm_s, l, depth, kv_s, F32)
        kv4 = [a.reshape(depth, bd, td * N_HEADS, HEAD_DIM) for a in kv_s]
        misc4 = misc_s.reshape(N_MISC, bd, td, GROUP_WIDTH)
        lo4 = lo_s.reshape(N_LO, bd, td, GROUP_WIDTH)
        s_sb = _sb_decode(kv4, lo4, c_sb_k, c_sb_v, page_table, l)
        s_hg, hg_new_s, s_cv, cv_new_s = _small_decode(misc4, lo4, state_hgrn, state_conv, hgrn_lb_logits,
                                                       hgrn_out_norm, conv_w, l)
        s_mb = _mb_decode(kv4, misc4, lo4, c_mb_k, c_mb_v, page_table, l)
        mixed_s = tuple(a.reshape(ms, GROUP_WIDTH) for a in (s_sb, s_hg, s_cv, s_mb))
        xs, hs = _out_project(mixed_s, w_out_b, xs, norm_post, norm_pre, l, tm_s)
        states_s[0].append(hg_new_s)
        states_s[1].append(cv_new_s)

    heads_p = lambda a: a.reshape(depth, b, t, N_HEADS, HEAD_DIM)
    heads_s = lambda a: a.reshape(depth, bd, td, N_HEADS, HEAD_DIM)
    return (xp.reshape(b, t, d), xs.reshape(bd, td, d),
            *(heads_p(a) for a in kv_p), jnp.stack(states_p[0]), jnp.stack(states_p[1]),
            *(heads_s(a) for a in kv_s), jnp.stack(states_s[0]), jnp.stack(states_s[1]))
```

```python
import functools

import jax
import jax.numpy as jnp
from jax import lax
from jax.experimental import pallas as pl
from jax.experimental.pallas import tpu as pltpu

F32 = jnp.float32
BF16 = jnp.bfloat16

HEAD_DIM = 128
N_HEADS = 4
GROUP_WIDTH = N_HEADS * HEAD_DIM
N_PARTS = 16
CONV_WIDTH = 3
MOBA_BLOCK = 256
MOBA_TOPK = 3
ROPE_THETA = 500000.0
ROPE_HALF = HEAD_DIM // 8
RMS_EPS = 1e-6
SCALE = HEAD_DIM ** -0.5
NEG_INF = float("-inf")

(P_SB_Q, P_SB_K, P_SB_V, P_SB_G, P_HG_Q, P_HG_F, P_HG_I, P_HG_G,
 P_CV_U, P_CV_B, P_CV_C, P_CV_G, P_MB_Q, P_MB_K, P_MB_V, P_MB_G) = range(N_PARTS)

STEP_PARTS = (P_SB_K, P_SB_V, P_MB_K, P_MB_V,
              P_HG_F, P_HG_Q, P_HG_I, P_MB_Q,
              P_SB_Q, P_SB_G, P_HG_G, P_CV_U, P_CV_B, P_CV_C, P_CV_G, P_MB_G)
N_KV, N_MISC, N_LO = 4, 4, 8
KV_SB_K, KV_SB_V, KV_MB_K, KV_MB_V = range(N_KV)
MISC_HG_F, MISC_HG_Q, MISC_HG_I, MISC_MB_Q = range(N_MISC)
LO_SB_Q, LO_SB_G, LO_HG_G, LO_CV_U, LO_CV_B, LO_CV_C, LO_CV_G, LO_MB_G = range(N_LO)
STEP_MB_K = STEP_PARTS.index(P_MB_K)
STEP_MB_Q = STEP_PARTS.index(P_MB_Q)

VMEM_LIMIT = 48 * 1024 * 1024


def _cparams(sem):
    return pltpu.CompilerParams(dimension_semantics=sem, vmem_limit_bytes=VMEM_LIMIT)


def _softplus(z):
    return jnp.maximum(z, 0.0) + jnp.log(1.0 + jnp.exp(-jnp.abs(z)))


def _silu(g):
    return g / (1.0 + jnp.exp(-g))


def _split2(x):
    hi = x.astype(BF16)
    lo = (x - hi.astype(F32)).astype(BF16)
    return hi, lo


def _split3(x):
    hi = x.astype(BF16)
    r = x - hi.astype(F32)
    mid = r.astype(BF16)
    lo = (r - mid.astype(F32)).astype(BF16)
    return hi, mid, lo


def _dot(a, b):
    return jnp.dot(a, b, preferred_element_type=F32)


def _dot_nt(a, b):
    return lax.dot_general(a, b, (((1,), (1,)), ((), ())), preferred_element_type=F32)


def _dot_tn(a, b):
    return lax.dot_general(a, b, (((0,), (0,)), ((), ())), preferred_element_type=F32)


def _layer_rows(p):
    return p.reshape(p.shape[0], 1, p.shape[1])


def _layer_row_spec(layer, width, col=None):
    return pl.BlockSpec((None, 1, width), lambda *idx: (layer, 0, 0 if col is None else col(*idx)))


def _head_rows(start, n):
    return pl.ds(start, n, stride=N_HEADS)


def _prenorm_kernel(x_ref, g_ref, h_ref):
    x = x_ref[...]
    ms = jnp.mean(x * x, axis=-1, keepdims=True)
    h_ref[...] = (x * lax.rsqrt(ms + RMS_EPS) * g_ref[...]).astype(h_ref.dtype)


def _prenorm(x, norm_pre, layer, tm):
    m, d = x.shape
    return pl.pallas_call(
        _prenorm_kernel,
        grid=(m // tm,),
        in_specs=[pl.BlockSpec((tm, d), lambda i: (i, 0)), _layer_row_spec(layer, d)],
        out_specs=pl.BlockSpec((tm, d), lambda i: (i, 0)),
        out_shape=jax.ShapeDtypeStruct((m, d), BF16),
        compiler_params=_cparams(("parallel",)),
        name="prenorm",
    )(x, _layer_rows(norm_pre))


def _proj_kernel(parts_ref, h_ref, w_ref, c_ref, s1_ref, s2_ref, *rest, tm, n_alias):
    kv_refs = rest[n_alias:n_alias + N_KV]
    misc_ref, lo_ref = rest[n_alias + N_KV:]
    n = pl.program_id(1)

    def head(acc, hh, rope):
        a = acc[:, hh * HEAD_DIM:(hh + 1) * HEAD_DIM]
        if rope:
            a = (a * c_ref[...]
                 + pltpu.roll(a, HEAD_DIM - ROPE_HALF, 1) * s1_ref[...]
                 + pltpu.roll(a, ROPE_HALF, 1) * s2_ref[...])
        return a

    @pl.when(n < N_KV)
    def _():
        acc = _dot(h_ref[...], w_ref[...])
        for a in range(N_KV):
            @pl.when(n == a)
            def _(a=a):
                for hh in range(N_HEADS):
                    kv_refs[a][0, _head_rows(hh, tm), :] = head(acc, hh, a == STEP_MB_K)

    @pl.when(jnp.logical_and(jnp.logical_and(n >= N_KV, n < N_KV + N_MISC), n != STEP_MB_Q))
    def _():
        misc_ref[0] = _dot(h_ref[...], w_ref[...])

    @pl.when(n == STEP_MB_Q)
    def _():
        acc = _dot(h_ref[...], w_ref[...])
        for hh in range(N_HEADS):
            misc_ref[0, :, hh * HEAD_DIM:(hh + 1) * HEAD_DIM] = head(acc, hh, True)

    @pl.when(n >= N_KV + N_MISC)
    def _():
        lo_ref[0] = _dot(h_ref[...], w_ref[...]).astype(lo_ref.dtype)


def _rope_tables(pos):
    inv_freq = ROPE_THETA ** (-jnp.arange(ROPE_HALF, dtype=F32) / ROPE_HALF)
    ang = pos.astype(F32)[:, None] * inv_freq[None, :]
    cos, sin = jnp.cos(ang), jnp.sin(ang)
    n = pos.shape[0]
    rest = HEAD_DIM - 2 * ROPE_HALF
    c = jnp.concatenate([cos, cos, jnp.ones((n, rest), F32)], axis=1)
    s1 = jnp.concatenate([-sin, jnp.zeros((n, HEAD_DIM - ROPE_HALF), F32)], axis=1)
    s2 = jnp.concatenate([jnp.zeros((n, ROPE_HALF), F32), sin, jnp.zeros((n, rest), F32)], axis=1)
    return c, s1, s2


def _project(h, w_bf16, tables, tm, layer, depth, kv_prev, lo_dtype):
    m, d = h.shape
    period_blocks = tables[0].shape[0] // tm
    tab_spec = pl.BlockSpec((tm, HEAD_DIM), lambda i, n, parts: (i % period_blocks, 0))
    n_alias = len(kv_prev)
    clamp = lambda v, hi: jnp.minimum(jnp.maximum(v, 0), hi)
    kv_spec = pl.BlockSpec((1, tm * N_HEADS, HEAD_DIM), lambda i, n, parts: (layer, i, 0))
    outs = pl.pallas_call(
        functools.partial(_proj_kernel, tm=tm, n_alias=n_alias),
        grid_spec=pltpu.PrefetchScalarGridSpec(
            num_scalar_prefetch=1,
            grid=(m // tm, N_PARTS),
            in_specs=[
                pl.BlockSpec((tm, d), lambda i, n, parts: (i, 0)),
                pl.BlockSpec((None, d, GROUP_WIDTH), lambda i, n, parts: (layer, 0, parts[n])),
                tab_spec, tab_spec, tab_spec,
            ] + [pl.BlockSpec(memory_space=pl.ANY)] * n_alias,
            out_specs=[kv_spec] * N_KV + [
                pl.BlockSpec((1, tm, GROUP_WIDTH),
                             lambda i, n, parts: (clamp(n - N_KV, N_MISC - 1), i, 0)),
                pl.BlockSpec((1, tm, GROUP_WIDTH),
                             lambda i, n, parts: (clamp(n - N_KV - N_MISC, N_LO - 1), i, 0)),
            ],
        ),
        out_shape=[jax.ShapeDtypeStruct((depth, m * N_HEADS, HEAD_DIM), F32)] * N_KV + [
            jax.ShapeDtypeStruct((N_MISC, m, GROUP_WIDTH), F32),
            jax.ShapeDtypeStruct((N_LO, m, GROUP_WIDTH), lo_dtype),
        ],
        input_output_aliases={6 + a: a for a in range(n_alias)},
        compiler_params=_cparams(("parallel", "arbitrary")),
        name="in_proj",
    )(jnp.array(STEP_PARTS, jnp.int32), h, w_bf16, *tables, *kv_prev)
    return list(outs[:N_KV]), outs[N_KV], outs[N_KV + 1]


def _out_kernel(a_ref, b_ref, c_ref, d_ref, w_ref, x_ref, g_ref, *rest):
    acc = _dot(a_ref[...].astype(BF16), w_ref[0])
    acc += _dot(b_ref[...].astype(BF16), w_ref[1])
    acc += _dot(c_ref[...].astype(BF16), w_ref[2])
    acc += _dot(d_ref[...].astype(BF16), w_ref[3])
    ms = jnp.mean(acc * acc, axis=-1, keepdims=True)
    y = x_ref[...] + acc * lax.rsqrt(ms + RMS_EPS) * g_ref[...]
    if len(rest) == 1:
        rest[0][...] = y
    else:
        gn_ref, o_ref, h_ref = rest
        o_ref[...] = y
        ms = jnp.mean(y * y, axis=-1, keepdims=True)
        h_ref[...] = (y * lax.rsqrt(ms + RMS_EPS) * gn_ref[...]).astype(h_ref.dtype)


def _out_project(mixed, w_out, x, norm_post, norm_pre, layer, tm):
    m, d = x.shape
    depth = w_out.shape[0]
    last = layer == depth - 1
    mix_spec = pl.BlockSpec((tm, GROUP_WIDTH), lambda i: (i, 0))
    row_spec = pl.BlockSpec((tm, d), lambda i: (i, 0))
    outs = pl.pallas_call(
        _out_kernel,
        grid=(m // tm,),
        in_specs=[mix_spec, mix_spec, mix_spec, mix_spec,
                  pl.BlockSpec((None, 4, GROUP_WIDTH, d), lambda i: (layer, 0, 0, 0)),
                  row_spec, _layer_row_spec(layer, d)]
        + ([] if last else [_layer_row_spec(layer + 1, d)]),
        out_specs=row_spec if last else [row_spec, row_spec],
        out_shape=(jax.ShapeDtypeStruct((m, d), F32) if last else
                   [jax.ShapeDtypeStruct((m, d), F32), jax.ShapeDtypeStruct((m, d), BF16)]),
        compiler_params=_cparams(("parallel",)),
        name="out_proj",
    )(*mixed, w_out, x, _layer_rows(norm_post), *([] if last else [_layer_rows(norm_pre)]))
    return (outs, None) if last else tuple(outs)


def _strict_upper_ones(n):
    r = lax.broadcasted_iota(jnp.int32, (n, n), 0)
    c = lax.broadcasted_iota(jnp.int32, (n, n), 1)
    return jnp.where(r > c, 1.0, 0.0).astype(BF16)


def _reverse_cumsum_groups(lk, run, tri):
    gw = tri.shape[0]
    ng = lk.shape[1] // gw
    rows = lk.shape[0]
    groups = [lk[:, gi * gw:(gi + 1) * gw] for gi in range(ng)]
    hi, lo = _split2(jnp.concatenate(groups, axis=0))
    local = _dot(hi, tri) + _dot(lo, tri)
    pieces = [None] * ng
    for gi in range(ng - 1, -1, -1):
        pieces[gi] = local[gi * rows:(gi + 1) * rows] + run
        run = run + jnp.sum(groups[gi], axis=-1, keepdims=True)
    return jnp.concatenate(pieces, axis=1), run


def _sb_prompt_kernel(q_ref, k_ref, v_ref, g_ref, o_ref, kb_ref, vb_ref, *, tq, t):
    h = pl.program_id(1)
    qi = pl.program_id(2)

    @pl.when(qi == 0)
    def _():
        for n in range(t // tq):
            rows = _head_rows(n * tq * N_HEADS + h, tq)
            kb_ref[n * tq:(n + 1) * tq, :] = k_ref[0, rows, :].astype(BF16)
            vb_ref[n * tq:(n + 1) * tq, :] = v_ref[0, rows, :].astype(BF16)

    q = q_ref[0]
    tri = _strict_upper_ones(HEAD_DIM)

    def block(kb, run, diagonal):
        k0 = kb * tq
        z = _dot_nt(q, kb_ref[pl.ds(k0, tq), :]) * SCALE
        lk = -_softplus(z)
        if diagonal:
            mask = (lax.broadcasted_iota(jnp.int32, (tq, tq), 1)
                    < lax.broadcasted_iota(jnp.int32, (tq, tq), 0))
            lk = jnp.where(mask, lk, 0.0)
        between, run = _reverse_cumsum_groups(lk, run, tri)
        w = jnp.exp(z + lk + between)
        if diagonal:
            w = jnp.where(mask, w, 0.0)
        return _dot(w.astype(BF16), vb_ref[pl.ds(k0, tq), :]), run

    def tile(n_past):
        acc, run = block(n_past, jnp.zeros((tq, 1), F32), True)
        for kb in range(n_past - 1, -1, -1):
            part, run = block(kb, run, False)
            acc = acc + part
        o_ref[...] = (acc * _silu(g_ref[0].astype(F32))).astype(o_ref.dtype)

    for n_past in range(t // tq):
        pl.when(qi == n_past)(functools.partial(tile, n_past))


def _sb_prompt(kv, lo, layer, b, t, tq=512):
    tq = min(tq, t)
    nq = t // tq
    qspec = lambda part: pl.BlockSpec((1, tq, HEAD_DIM), lambda bi, h, i: (part, bi * nq + i, h))
    kvspec = pl.BlockSpec((1, t * N_HEADS, HEAD_DIM), lambda bi, h, i: (layer, bi, 0))
    return pl.pallas_call(
        functools.partial(_sb_prompt_kernel, tq=tq, t=t),
        grid=(b, N_HEADS, nq),
        in_specs=[qspec(LO_SB_Q), kvspec, kvspec, qspec(LO_SB_G)],
        out_specs=pl.BlockSpec((tq, HEAD_DIM), lambda bi, h, i: (bi * nq + i, h)),
        out_shape=jax.ShapeDtypeStruct((b * t, GROUP_WIDTH), BF16),
        scratch_shapes=[pltpu.VMEM((t, HEAD_DIM), BF16), pltpu.VMEM((t, HEAD_DIM), BF16)],
        compiler_params=_cparams(("parallel", "arbitrary", "arbitrary")),
        name="sb_prompt",
    )(lo, kv[KV_SB_K], kv[KV_SB_V], lo)


def _lower_bound_logs(lb_logits, layer):
    depth = lb_logits.shape[0]
    rows = [lb_logits[i:i + 1, :] for i in range(depth)]
    mx = functools.reduce(jnp.maximum, rows)
    es = [jnp.exp(r - mx) for r in rows]
    tot = functools.reduce(lambda a, c: a + c, es)
    ps = [e / tot for e in es]
    cs = ps[0]
    for i in range(1, layer + 1):
        cs = cs + ps[i]
    lb = jnp.maximum(cs - ps[0], 0.0)
    return jnp.log(lb), jnp.log1p(-lb)


def _log_forget(x, log_lb, log_1m_lb):
    a = log_lb
    bb = log_1m_lb - _softplus(-x)
    return jnp.maximum(a, bb) + jnp.log1p(jnp.exp(-jnp.abs(a - bb)))


def _col_bcast(row):
    n = row.shape[-1]
    return jnp.broadcast_to(row, (n, n)).T


def _head_rmsnorm(o, g_row):
    ms = jnp.mean(o * o, axis=-1, keepdims=True)
    return o * lax.rsqrt(ms + RMS_EPS) * g_row


def _hg_prompt_kernel(q_ref, f_ref, i_ref, g_ref, lb_ref, gn_ref, o_ref, s_out_ref, s_ref,
                      *, c, layer, heads):
    ci = pl.program_id(2)

    @pl.when(ci == 0)
    def _():
        s_ref[...] = jnp.zeros_like(s_ref)

    for hh in range(heads):
        _hg_prompt_head(q_ref, f_ref, i_ref, g_ref, lb_ref, gn_ref, o_ref, s_out_ref, s_ref,
                        hh, c=c, layer=layer)


def _hg_prompt_head(q_ref, f_ref, i_ref, g_ref, lb_ref, gn_ref, o_ref, s_out_ref, s_ref, hh, *, c, layer):
    ci = pl.program_id(2)
    nc = pl.num_programs(2)
    d = HEAD_DIM
    sl = slice(hh * d, (hh + 1) * d)

    log_lb, log_1m_lb = _lower_bound_logs(lb_ref[:, sl], layer)
    g = _log_forget(f_ref[0, :, sl], log_lb, log_1m_lb)
    kk = 1.0 - jnp.exp(g)
    qs = q_ref[0, :, sl] * SCALE
    v = i_ref[0, :, sl]
    vb = v.astype(BF16)

    r = lax.broadcasted_iota(jnp.int32, (c, c), 0)
    cc = lax.broadcasted_iota(jnp.int32, (c, c), 1)
    lower = jnp.where(r >= cc, 1.0, 0.0).astype(BF16)
    g_hi, g_mid, g_lo = _split3(g)
    gc = _dot(lower, g_hi) + _dot(lower, g_mid) + _dot(lower, g_lo)
    g_last = gc[c - 1:c, :]

    s = s_ref[hh]
    o = _dot((qs * jnp.exp(gc)).astype(BF16), s.astype(BF16))

    a = jnp.zeros((c, c), F32)
    blk = 8
    while blk < c:
        nb = c // blk
        gc3 = gc.reshape(nb, blk, d)
        g_end = gc3[:, blk - 1:blk, :]
        g_prev = jnp.concatenate([jnp.zeros((1, 1, d), F32), g_end[:nb - 1]], axis=0)
        qd = (qs.reshape(nb, blk, d) * jnp.exp(gc3 - g_prev)).reshape(c, d)
        kd = (kk.reshape(nb, blk, d) * jnp.exp(g_end - gc3)).reshape(c, d)
        sc = _dot_nt(qd.astype(BF16), kd.astype(BF16))
        rb = r // blk
        cb = cc // blk
        pair = jnp.logical_and(rb % 2 == 1, cb == rb - 1)
        a = jnp.where(pair, sc, a)
        blk *= 2
    o = o + _dot(a.astype(BF16), vb)

    n8 = c // 8
    gc8 = gc.reshape(n8, 8, d)
    q8 = qs.reshape(n8, 8, d)
    k8 = kk.reshape(n8, 8, d)
    v8 = v.reshape(n8, 8, d)
    sub = lax.broadcasted_iota(jnp.int32, (n8, 8, d), 1)
    od = jnp.zeros((n8, 8, d), F32)
    for j in range(8):
        diff = gc8[:, j:j + 1, :] - gc8
        e = jnp.exp(jnp.where(sub <= j, diff, NEG_INF))
        sc = jnp.sum(e * k8 * q8[:, j:j + 1, :], axis=-1, keepdims=True)
        oj = jnp.sum(sc * v8, axis=1, keepdims=True)
        od = jnp.where(sub == j, oj, od)
    o = o + od.reshape(c, d)

    kd = (kk * jnp.exp(g_last - gc)).astype(BF16)
    s_new = _col_bcast(jnp.exp(g_last)) * s + _dot_tn(kd, vb)
    s_ref[hh] = s_new

    @pl.when(ci == nc - 1)
    def _():
        s_out_ref[0, hh] = s_new

    gate = g_ref[0, :, sl].astype(F32)
    o_ref[:, sl] = (_head_rmsnorm(o, gn_ref[:, sl]) * _silu(gate)).astype(o_ref.dtype)


def _hg_prompt(misc, lo, lb_logits, g_hg, layer, b, t, c=256, heads=N_HEADS):
    c = min(c, t)
    nc = t // c
    depth = lb_logits.shape[0]
    hw = heads * HEAD_DIM
    spec = lambda part: pl.BlockSpec((1, c, hw), lambda bi, h, i: (part, bi * nc + i, h))
    return pl.pallas_call(
        functools.partial(_hg_prompt_kernel, c=c, layer=layer, heads=heads),
        grid=(b, N_HEADS // heads, nc),
        in_specs=[spec(MISC_HG_Q), spec(MISC_HG_F), spec(MISC_HG_I), spec(LO_HG_G),
                  pl.BlockSpec((depth, hw), lambda bi, h, i: (0, h)),
                  _layer_row_spec(layer, hw, col=lambda bi, h, i: h)],
        out_specs=[pl.BlockSpec((c, hw), lambda bi, h, i: (bi * nc + i, h)),
                   pl.BlockSpec((1, heads, HEAD_DIM, HEAD_DIM), lambda bi, h, i: (bi, h, 0, 0))],
        out_shape=[jax.ShapeDtypeStruct((b * t, GROUP_WIDTH), BF16),
                   jax.ShapeDtypeStruct((b, N_HEADS, HEAD_DIM, HEAD_DIM), F32)],
        scratch_shapes=[pltpu.VMEM((heads, HEAD_DIM, HEAD_DIM), F32)],
        compiler_params=_cparams(("parallel", "parallel", "arbitrary")),
        name="hgrn_prompt",
    )(misc, misc, misc, lo, lb_logits, _layer_rows(g_hg))


def _conv_prompt_kernel(u_ref, b_ref, c_ref, g_ref, w_ref, o_ref, st_ref, carry_ref, *, tt):
    ti = pl.program_id(1)
    nt = pl.num_programs(1)

    @pl.when(ti == 0)
    def _():
        carry_ref[...] = jnp.zeros_like(carry_ref)

    z = c_ref[0].astype(F32) * u_ref[0].astype(F32)
    row = lax.broadcasted_iota(jnp.int32, z.shape, 0)
    p0 = carry_ref[0:1, :]
    p1 = carry_ref[1:2, :]
    z1 = jnp.where(row == 0, p1, pltpu.roll(z, 1, 0))
    z2 = jnp.where(row == 0, p0, jnp.where(row == 1, p1, pltpu.roll(z, 2, 0)))
    y = z2 * w_ref[0:1, :] + z1 * w_ref[1:2, :] + z * w_ref[2:3, :]
    o_ref[...] = (b_ref[0].astype(F32) * y * _silu(g_ref[0].astype(F32))).astype(o_ref.dtype)
    tail = z[tt - 2:tt, :]
    carry_ref[0:2, :] = tail

    @pl.when(ti == nt - 1)
    def _():
        st_ref[0] = tail


def _conv_prompt(lo, conv_w, layer, b, t, tt=512):
    tt = min(tt, t)
    nt = t // tt
    spec = lambda part: pl.BlockSpec((1, tt, GROUP_WIDTH), lambda bi, i: (part, bi * nt + i, 0))
    return pl.pallas_call(
        functools.partial(_conv_prompt_kernel, tt=tt),
        grid=(b, nt),
        in_specs=[spec(LO_CV_U), spec(LO_CV_B), spec(LO_CV_C), spec(LO_CV_G),
                  pl.BlockSpec((None, CONV_WIDTH, GROUP_WIDTH), lambda bi, i: (layer, 0, 0))],
        out_specs=[pl.BlockSpec((tt, GROUP_WIDTH), lambda bi, i: (bi * nt + i, 0)),
                   pl.BlockSpec((1, CONV_WIDTH - 1, GROUP_WIDTH), lambda bi, i: (bi, 0, 0))],
        out_shape=[jax.ShapeDtypeStruct((b * t, GROUP_WIDTH), BF16),
                   jax.ShapeDtypeStruct((b, CONV_WIDTH - 1, GROUP_WIDTH), F32)],
        scratch_shapes=[pltpu.VMEM((8, GROUP_WIDTH), F32)],
        compiler_params=_cparams(("parallel", "arbitrary")),
        name="conv_prompt",
    )(lo, lo, lo, lo, conv_w)


def _lane_pick(mat, idx):
    lane = lax.broadcasted_iota(jnp.int32, mat.shape, 1)
    return jnp.max(jnp.where(lane == idx, mat, NEG_INF), axis=-1, keepdims=True)


def _moba_select_t(gate_t, n_past, n_blocks):
    row = lax.broadcasted_iota(jnp.int32, gate_t.shape, 0)
    gm = jnp.where(row < n_past, gate_t, NEG_INF)
    sel = jnp.full(gate_t.shape, NEG_INF, F32)
    for n in range(n_blocks):
        cur = gm[n:n + 1, :]
        beats = jnp.where(gm > cur, 1.0, jnp.where(jnp.logical_and(gm == cur, row < n), 1.0, 0.0))
        rank = jnp.sum(beats, axis=0, keepdims=True)
        ok = jnp.logical_and(rank < MOBA_TOPK, cur > NEG_INF)
        sel = jnp.where(jnp.logical_and(row == n, ok), 0.0, sel)
    return sel


def _mb_prompt_kernel(q_ref, k_ref, v_ref, g_ref, o_ref, kmean_ref, kb_ref, vt_ref, *, nb, qt):
    h = pl.program_id(1)
    qi = pl.program_id(2)
    blk = MOBA_BLOCK

    @pl.when(qi == 0)
    def _():
        kmean_ref[...] = jnp.zeros_like(kmean_ref)
        for n in range(nb):
            rows = _head_rows(n * blk * N_HEADS + h, blk)
            kn = k_ref[0, rows, :]
            kmean_ref[n:n + 1, :] = jnp.mean(kn, axis=0, keepdims=True)
            kb_ref[n * blk:(n + 1) * blk, :] = kn.astype(BF16)
            vt_ref[:, n * blk:(n + 1) * blk] = v_ref[0, rows, :].T.astype(BF16)

    q = q_ref[0]
    qb = q.astype(BF16)
    per = qt // blk
    gate_t = lax.dot_general(kmean_ref[...], q, (((1,), (1,)), ((), ())),
                             precision=lax.Precision.HIGHEST, preferred_element_type=F32)
    own = qi * per + lax.broadcasted_iota(jnp.int32, (1, qt), 1) // blk
    sel_t = _moba_select_t(gate_t, own, nb)

    rel = (lax.broadcasted_iota(jnp.int32, (blk, qt), 0)
           - lax.broadcasted_iota(jnp.int32, (blk, qt), 1))

    def attend(n_blk):
        parts = []
        for j in range(n_blk):
            s = (_dot_nt(kb_ref[j * blk:(j + 1) * blk, :], qb) * SCALE
                 + jnp.where(j < own, sel_t[j:j + 1, :], 0.0))
            parts.append(jnp.where(rel <= (qi * per - j) * blk, s, NEG_INF))
        m = functools.reduce(jnp.maximum, [jnp.max(s, axis=0, keepdims=True) for s in parts])
        ps = [jnp.exp(s - m) for s in parts]
        l = functools.reduce(lambda a, c: a + c, [jnp.sum(p, axis=0, keepdims=True) for p in ps])
        p_all = jnp.concatenate([p.astype(BF16) for p in ps], axis=0)
        acc = _dot(vt_ref[:, :n_blk * blk], p_all)
        o_ref[...] = ((acc / l).T * _silu(g_ref[0].astype(F32))).astype(o_ref.dtype)

    for tile in range(nb // per):
        pl.when(qi == tile)(functools.partial(attend, per * (tile + 1)))


def _mb_prompt(kv, misc, lo, layer, b, t, qt=512):
    blk = MOBA_BLOCK
    nb = t // blk
    nb_pad = -(-nb // 8) * 8
    qt = min(qt, t)
    nq = t // qt
    qspec = lambda part: pl.BlockSpec((1, qt, HEAD_DIM), lambda bi, h, i: (part, bi * nq + i, h))
    kvspec = pl.BlockSpec((1, t * N_HEADS, HEAD_DIM), lambda bi, h, i: (layer, bi, 0))
    return pl.pallas_call(
        functools.partial(_mb_prompt_kernel, nb=nb, qt=qt),
        grid=(b, N_HEADS, nq),
        in_specs=[qspec(MISC_MB_Q), kvspec, kvspec, qspec(LO_MB_G)],
        out_specs=pl.BlockSpec((qt, HEAD_DIM), lambda bi, h, i: (bi * nq + i, h)),
        out_shape=jax.ShapeDtypeStruct((b * t, GROUP_WIDTH), BF16),
        scratch_shapes=[pltpu.VMEM((nb_pad, HEAD_DIM), F32), pltpu.VMEM((t, HEAD_DIM), BF16),
                        pltpu.VMEM((HEAD_DIM, t), BF16)],
        compiler_params=_cparams(("parallel", "arbitrary", "arbitrary")),
        name="moba_prompt",
    )(misc, kv[KV_MB_K], kv[KV_MB_V], lo)


def _stack_heads(x):
    return jnp.concatenate([x[:, h * HEAD_DIM:(h + 1) * HEAD_DIM] for h in range(N_HEADS)], axis=0)


def _spread_row(x, j, t):
    return jnp.concatenate(
        [jnp.broadcast_to(x[j * N_HEADS + h:j * N_HEADS + h + 1, :], (t, HEAD_DIM))
         for h in range(N_HEADS)], axis=0)


def _store_heads(o_ref, acc, gate, t):
    for h in range(N_HEADS):
        sl = slice(h * HEAD_DIM, (h + 1) * HEAD_DIM)
        o_ref[0, :, sl] = acc[h * t:(h + 1) * t, :] * _silu(gate[:, sl])


def _own_head(rows, cols, t):
    r = lax.broadcasted_iota(jnp.int32, (rows, cols), 0)
    c = lax.broadcasted_iota(jnp.int32, (rows, cols), 1)
    return c % N_HEADS == r // t


def _page_specs(layer, n_steps, n_group, page_rows, reverse):
    specs = []
    for gi in range(n_group):
        if reverse:
            imap = lambda bi, s, pt, gi=gi: (layer, pt[bi, (n_steps - 1 - s) * n_group + gi], 0, 0)
        else:
            imap = lambda bi, s, pt, gi=gi: (layer, pt[bi, s * n_group + gi], 0, 0)
        specs.append(pl.BlockSpec((None, None, page_rows, HEAD_DIM), imap))
    return specs


def _tok_spec(part, t):
    return pl.BlockSpec((1, 1, t, GROUP_WIDTH), lambda bi, *_: (part, bi, 0, 0))


def _new_kv_spec(layer, t):
    return pl.BlockSpec((1, 1, t * N_HEADS, HEAD_DIM), lambda bi, *_: (layer, bi, 0, 0))


def _sb_decode_kernel(pt_ref, q_ref, kn_ref, vn_ref, g_ref, *rest, t, n_group):
    k_refs = rest[:n_group]
    v_refs = rest[n_group:2 * n_group]
    o_ref, acc_ref, run_ref, qr_ref = rest[2 * n_group:]
    s = pl.program_id(1)
    ns = pl.num_programs(1)
    pr = k_refs[0].shape[0]
    rows = N_HEADS * t

    @pl.when(s == 0)
    def _():
        qr = _stack_heads(q_ref[0, 0])
        qr_ref[...] = qr
        tq = lax.broadcasted_iota(jnp.int32, (rows, 1), 0) % t
        kn = kn_ref[0, 0]
        vn = vn_ref[0, 0]
        acc = jnp.zeros((rows, HEAD_DIM), F32)
        run = jnp.zeros((rows, 1), F32)
        for j in range(t - 1, -1, -1):
            z = jnp.sum(qr * _spread_row(kn, j, t), axis=-1, keepdims=True) * SCALE
            ok = tq > j
            lk = jnp.where(ok, -_softplus(z), 0.0)
            w = jnp.where(ok, jnp.exp(z + lk + run), 0.0)
            acc = acc + w * _spread_row(vn, j, t)
            run = run + lk
        acc_ref[...] = acc
        run_ref[...] = jnp.broadcast_to(run, run_ref.shape)

    qb = qr_ref[...].astype(BF16)
    tri = _strict_upper_ones(HEAD_DIM)
    own = _own_head(rows, n_group * pr, t)
    z = jnp.concatenate([_dot_nt(qb, k_refs[gi][...].astype(BF16)) for gi in range(n_group)],
                        axis=1) * SCALE
    lk = jnp.where(own, -_softplus(z), 0.0)
    between, run = _reverse_cumsum_groups(lk, run_ref[:, 0:1], tri)
    w = jnp.where(own, jnp.exp(z + lk + between), 0.0).astype(BF16)
    acc = acc_ref[...]
    for gi in range(n_group):
        acc = acc + _dot(w[:, gi * pr:(gi + 1) * pr], v_refs[gi][...].astype(BF16))
    acc_ref[...] = acc
    run_ref[...] = jnp.broadcast_to(run, run_ref.shape)

    @pl.when(s == ns - 1)
    def _():
        _store_heads(o_ref, acc, g_ref[0, 0], t)


def _sb_decode(kv4, lo4, cache_k, cache_v, page_table, layer, n_group=16):
    _, bd, t, _ = lo4.shape
    n_pages = page_table.shape[1]
    pr = cache_k.shape[2]
    n_group = min(n_group, n_pages)
    n_steps = n_pages // n_group
    rows = N_HEADS * t
    grid_spec = pltpu.PrefetchScalarGridSpec(
        num_scalar_prefetch=1,
        grid=(bd, n_steps),
        in_specs=[_tok_spec(LO_SB_Q, t), _new_kv_spec(layer, t), _new_kv_spec(layer, t), _tok_spec(LO_SB_G, t)]
        + _page_specs(layer, n_steps, n_group, pr, True)
        + _page_specs(layer, n_steps, n_group, pr, True),
        out_specs=pl.BlockSpec((1, t, GROUP_WIDTH), lambda bi, s, pt: (bi, 0, 0)),
        scratch_shapes=[pltpu.VMEM((rows, HEAD_DIM), F32),
                        pltpu.VMEM((rows, HEAD_DIM), F32),
                        pltpu.VMEM((rows, HEAD_DIM), F32)],
    )
    return pl.pallas_call(
        functools.partial(_sb_decode_kernel, t=t, n_group=n_group),
        grid_spec=grid_spec,
        out_shape=jax.ShapeDtypeStruct((bd, t, GROUP_WIDTH), F32),
        compiler_params=_cparams(("parallel", "arbitrary")),
        name="sb_decode",
    )(page_table, lo4, kv4[KV_SB_K], kv4[KV_SB_V], lo4, *([cache_k] * n_group), *([cache_v] * n_group))


def _mb_gate_kernel(pt_ref, q_ref, *rest, t, n_group):
    k_refs = rest[:n_group]
    gate_ref, logit_ref, qr_ref = rest[n_group:]
    s = pl.program_id(1)
    pr = k_refs[0].shape[0]
    per_blk = MOBA_BLOCK * N_HEADS // pr

    @pl.when(s == 0)
    def _():
        qr_ref[...] = _stack_heads(q_ref[0, 0])
        gate_ref[...] = jnp.full(gate_ref.shape, NEG_INF, F32)

    qr = qr_ref[...]
    qb = qr.astype(BF16)
    lane = lax.broadcasted_iota(jnp.int32, gate_ref.shape[1:], 1)
    gates = gate_ref[0]
    for bi in range(n_group // per_blk):
        ksum = jnp.zeros((8, HEAD_DIM), F32)
        for pi in range(per_blk):
            gi = bi * per_blk + pi
            k = k_refs[gi][...]
            ksum = ksum + jnp.sum(k.reshape(pr // 8, 8, HEAD_DIM), axis=0)
            logit_ref[0, :, gi * pr:(gi + 1) * pr] = _dot_nt(qb, k.astype(BF16))
        kmean = (ksum[0:N_HEADS] + ksum[N_HEADS:2 * N_HEADS]) * (1.0 / MOBA_BLOCK)
        kmean_rows = jnp.concatenate(
            [jnp.broadcast_to(kmean[h:h + 1], (t, HEAD_DIM)) for h in range(N_HEADS)], axis=0)
        col = jnp.sum(qr * kmean_rows, axis=-1, keepdims=True)
        gates = jnp.where(lane == s * (n_group // per_blk) + bi, col, gates)
    gate_ref[0] = gates


def _mb_attend_kernel(pt_ref, q_ref, kn_ref, vn_ref, g_ref, gate_ref, logit_ref, *rest,
                      t, n_group, n_past):
    v_refs = rest[:n_group]
    o_ref, acc_ref, m_ref, l_ref, sel_ref = rest[n_group:]
    s = pl.program_id(1)
    ns = pl.num_programs(1)
    pr = v_refs[0].shape[0]
    per_blk = MOBA_BLOCK * N_HEADS // pr
    rows = N_HEADS * t

    @pl.when(s == 0)
    def _():
        neg = lambda n: jnp.full((n, HEAD_DIM), NEG_INF, F32)
        nbp = -(-n_past // 8) * 8
        gate_t = jnp.concatenate([gate_ref[0], neg(HEAD_DIM - rows)], axis=0).T
        sel_t = _moba_select_t(gate_t[:nbp], n_past, n_past)
        sel_ref[...] = jnp.concatenate([sel_t, neg(HEAD_DIM - nbp)], axis=0).T[:rows]
        qr = _stack_heads(q_ref[0, 0])
        tq = lax.broadcasted_iota(jnp.int32, (rows, 1), 0) % t
        kn = kn_ref[0, 0]
        vn = vn_ref[0, 0]
        zs = []
        for j in range(t):
            z = jnp.sum(qr * _spread_row(kn, j, t), axis=-1, keepdims=True) * SCALE
            zs.append(jnp.where(tq >= j, z, NEG_INF))
        m = functools.reduce(jnp.maximum, zs)
        l = jnp.zeros((rows, 1), F32)
        acc = jnp.zeros((rows, HEAD_DIM), F32)
        for j in range(t):
            p = jnp.exp(zs[j] - m)
            l = l + p
            acc = acc + p * _spread_row(vn, j, t)
        acc_ref[...] = acc
        m_ref[...] = jnp.broadcast_to(m, m_ref.shape)
        l_ref[...] = jnp.broadcast_to(l, l_ref.shape)

    acc = acc_ref[...]
    m = m_ref[:, 0:1]
    l = l_ref[:, 0:1]
    sel = sel_ref[...]
    own = _own_head(rows, n_group * pr, t)
    n_blk = n_group // per_blk
    bias = jnp.concatenate(
        [jnp.broadcast_to(_lane_pick(sel, s * n_blk + bi), (rows, per_blk * pr)) for bi in range(n_blk)],
        axis=1)
    sc = jnp.where(own, logit_ref[0] * SCALE + bias, NEG_INF)
    m_new = jnp.maximum(m, jnp.max(sc, axis=-1, keepdims=True))
    alpha = jnp.exp(m - m_new)
    p = jnp.exp(sc - m_new)
    l = alpha * l + jnp.sum(p, axis=-1, keepdims=True)
    pb = p.astype(BF16)
    acc = alpha * acc
    for gi in range(n_group):
        acc = acc + _dot(pb[:, gi * pr:(gi + 1) * pr], v_refs[gi][...].astype(BF16))
    m = m_new
    acc_ref[...] = acc
    m_ref[...] = jnp.broadcast_to(m, m_ref.shape)
    l_ref[...] = jnp.broadcast_to(l, l_ref.shape)

    @pl.when(s == ns - 1)
    def _():
        _store_heads(o_ref, acc / l, g_ref[0, 0], t)


def _mb_decode(kv4, misc4, lo4, cache_k, cache_v, page_table, layer, n_group=16):
    _, bd, t, _ = lo4.shape
    n_pages = page_table.shape[1]
    pr = cache_k.shape[2]
    n_group = min(n_group, n_pages)
    n_steps = n_pages // n_group
    n_past = n_pages * pr // (MOBA_BLOCK * N_HEADS)
    rows = N_HEADS * t
    gate_spec = pl.BlockSpec((1, rows, HEAD_DIM), lambda bi, s, pt: (bi, 0, 0))
    logit_spec = pl.BlockSpec((1, rows, n_group * pr), lambda bi, s, pt: (bi, 0, s))

    gates, logits = pl.pallas_call(
        functools.partial(_mb_gate_kernel, t=t, n_group=n_group),
        grid_spec=pltpu.PrefetchScalarGridSpec(
            num_scalar_prefetch=1,
            grid=(bd, n_steps),
            in_specs=[_tok_spec(MISC_MB_Q, t)] + _page_specs(layer, n_steps, n_group, pr, False),
            out_specs=[gate_spec, logit_spec],
            scratch_shapes=[pltpu.VMEM((rows, HEAD_DIM), F32)],
        ),
        out_shape=[jax.ShapeDtypeStruct((bd, rows, HEAD_DIM), F32),
                   jax.ShapeDtypeStruct((bd, rows, n_pages * pr), F32)],
        compiler_params=_cparams(("parallel", "arbitrary")),
        name="moba_decode_gate",
    )(page_table, misc4, *([cache_k] * n_group))

    return pl.pallas_call(
        functools.partial(_mb_attend_kernel, t=t, n_group=n_group, n_past=n_past),
        grid_spec=pltpu.PrefetchScalarGridSpec(
            num_scalar_prefetch=1,
            grid=(bd, n_steps),
            in_specs=[_tok_spec(MISC_MB_Q, t), _new_kv_spec(layer, t), _new_kv_spec(layer, t),
                      _tok_spec(LO_MB_G, t), gate_spec, logit_spec]
            + _page_specs(layer, n_steps, n_group, pr, False),
            out_specs=pl.BlockSpec((1, t, GROUP_WIDTH), lambda bi, s, pt: (bi, 0, 0)),
            scratch_shapes=[pltpu.VMEM((rows, HEAD_DIM), F32),
                            pltpu.VMEM((rows, HEAD_DIM), F32),
                            pltpu.VMEM((rows, HEAD_DIM), F32),
                            pltpu.VMEM((rows, HEAD_DIM), F32)],
        ),
        out_shape=jax.ShapeDtypeStruct((bd, t, GROUP_WIDTH), F32),
        compiler_params=_cparams(("parallel", "arbitrary")),
        name="moba_decode_attend",
    )(page_table, misc4, kv4[KV_MB_K], kv4[KV_MB_V], lo4, gates, logits, *([cache_v] * n_group))


def _small_decode_kernel(hq_ref, hf_ref, hi_ref, hgate_ref, cu_ref, cb_ref, cc_ref, cg_ref,
                         s0_ref, cv0_ref, lb_ref, gn_ref, w_ref,
                         ohg_ref, s_out_ref, ocv_ref, cv_out_ref, *, t, layer):
    d = HEAD_DIM
    log_lb, log_1m_lb = _lower_bound_logs(lb_ref[...], layer)
    g_all = _log_forget(hf_ref[0, 0], log_lb, log_1m_lb)
    k_all = 1.0 - jnp.exp(g_all)
    q_all = hq_ref[0, 0] * SCALE
    v_all = hi_ref[0, 0]
    gate_all = hgate_ref[0, 0]
    gn = gn_ref[...]
    pad = jnp.zeros((8 - t % 8, d), F32) if t % 8 else None

    for h in range(N_HEADS):
        sl = slice(h * d, (h + 1) * d)
        g, kk, qs, v = g_all[:, sl], k_all[:, sl], q_all[:, sl], v_all[:, sl]
        s0 = s0_ref[0, h]
        gcs = []
        run = jnp.zeros((1, d), F32)
        for i in range(t):
            run = run + g[i:i + 1]
            gcs.append(run)
        qe = jnp.concatenate([qs[i:i + 1] * jnp.exp(gcs[i]) for i in range(t)], axis=0)
        if pad is not None:
            qe = jnp.concatenate([qe, pad], axis=0)
        inter = _dot(qe.astype(BF16), s0.astype(BF16))
        s_new = _col_bcast(jnp.exp(gcs[t - 1])) * s0
        for j in range(t):
            s_new = s_new + _col_bcast(kk[j:j + 1] * jnp.exp(gcs[t - 1] - gcs[j])) * v[j:j + 1]
        s_out_ref[0, h] = s_new
        for i in range(t):
            o = inter[i:i + 1]
            for j in range(i + 1):
                sc = jnp.sum(qs[i:i + 1] * kk[j:j + 1] * jnp.exp(gcs[i] - gcs[j]),
                             axis=-1, keepdims=True)
                o = o + sc * v[j:j + 1]
            o = _head_rmsnorm(o, gn[:, sl]) * _silu(gate_all[i:i + 1, sl])
            ohg_ref[0, i:i + 1, sl] = o

    z = cc_ref[0, 0] * cu_ref[0, 0]
    zz = [cv0_ref[0, j:j + 1, :] for j in range(CONV_WIDTH - 1)] + [z[i:i + 1] for i in range(t)]
    cb = cb_ref[0, 0]
    cg = cg_ref[0, 0]
    for i in range(t):
        y = zz[i] * w_ref[0:1, :]
        for j in range(1, CONV_WIDTH):
            y = y + zz[i + j] * w_ref[j:j + 1, :]
        ocv_ref[0, i:i + 1, :] = cb[i:i + 1] * y * _silu(cg[i:i + 1])
    for j in range(CONV_WIDTH - 1):
        cv_out_ref[0, j:j + 1, :] = zz[t + j]


def _small_decode(misc4, lo4, state_hgrn, state_conv, lb_logits, g_hg, conv_w, layer):
    _, bd, t, _ = lo4.shape
    depth = lb_logits.shape[0]
    d = HEAD_DIM
    return pl.pallas_call(
        functools.partial(_small_decode_kernel, t=t, layer=layer),
        grid=(bd,),
        in_specs=[_tok_spec(MISC_HG_Q, t), _tok_spec(MISC_HG_F, t), _tok_spec(MISC_HG_I, t),
                  _tok_spec(LO_HG_G, t), _tok_spec(LO_CV_U, t), _tok_spec(LO_CV_B, t),
                  _tok_spec(LO_CV_C, t), _tok_spec(LO_CV_G, t),
                  pl.BlockSpec((None, 1, N_HEADS, d, d), lambda bi: (layer, bi, 0, 0, 0)),
                  pl.BlockSpec((None, 1, CONV_WIDTH - 1, GROUP_WIDTH), lambda bi: (layer, bi, 0, 0)),
                  pl.BlockSpec((depth, GROUP_WIDTH), lambda bi: (0, 0)),
                  _layer_row_spec(layer, GROUP_WIDTH),
                  pl.BlockSpec((None, CONV_WIDTH, GROUP_WIDTH), lambda bi: (layer, 0, 0))],
        out_specs=[pl.BlockSpec((1, t, GROUP_WIDTH), lambda bi: (bi, 0, 0)),
                   pl.BlockSpec((1, N_HEADS, d, d), lambda bi: (bi, 0, 0, 0)),
                   pl.BlockSpec((1, t, GROUP_WIDTH), lambda bi: (bi, 0, 0)),
                   pl.BlockSpec((1, CONV_WIDTH - 1, GROUP_WIDTH), lambda bi: (bi, 0, 0))],
        out_shape=[jax.ShapeDtypeStruct((bd, t, GROUP_WIDTH), F32),
                   jax.ShapeDtypeStruct((bd, N_HEADS, d, d), F32),
                   jax.ShapeDtypeStruct((bd, t, GROUP_WIDTH), F32),
                   jax.ShapeDtypeStruct((bd, CONV_WIDTH - 1, GROUP_WIDTH), F32)],
        compiler_params=_cparams(("parallel",)),
        name="hgrn_conv_decode",
    )(misc4, misc4, misc4, lo4, lo4, lo4, lo4, lo4,
      state_hgrn, state_conv, lb_logits, _layer_rows(g_hg), conv_w)


def _row_tile(m, pref):
    return pref if m % pref == 0 else m


def kernel(x_prompt, x_sample, cache_sb_k, cache_sb_v, cache_moba_k, cache_moba_v, state_hgrn, state_conv,
           page_table, w_in, w_out, norm_pre, norm_post, hgrn_out_norm, conv_w, hgrn_lb_logits):
    b, t, d = x_prompt.shape
    bd, td, _ = x_sample.shape
    depth = w_in.shape[0]
    n_phys, ps = cache_sb_k.shape[1], cache_sb_k.shape[2]
    past_len = page_table.shape[1] * ps
    assert past_len % MOBA_BLOCK == 0 and td <= MOBA_BLOCK and t % MOBA_BLOCK == 0

    mp, ms = b * t, bd * td
    tm_p, tm_s = _row_tile(mp, 1024), _row_tile(ms, 1024)
    tm_o = _row_tile(mp, 512)
    tab_p = _rope_tables(jnp.arange(t, dtype=jnp.int32))
    pos_s = past_len + jnp.arange(td, dtype=jnp.int32)
    tab_s = tuple(jnp.tile(a, (bd, 1)) for a in _rope_tables(pos_s))
    assert t % tm_p == 0 or tm_p % t == 0
    if tm_p > t:
        tab_p = tuple(jnp.tile(a, (tm_p // t, 1)) for a in tab_p)

    w_steps = w_in.astype(BF16)
    w_out_b = w_out.astype(BF16).reshape(depth, 4, GROUP_WIDTH, d)
    pool = lambda c: c.reshape(depth, n_phys, ps * N_HEADS, HEAD_DIM)
    c_sb_k, c_sb_v, c_mb_k, c_mb_v = pool(cache_sb_k), pool(cache_sb_v), pool(cache_moba_k), pool(cache_moba_v)

    xp = x_prompt.reshape(mp, d)
    xs = x_sample.reshape(ms, d)
    kv_p, kv_s = [], []
    states_p = [[] for _ in range(2)]
    states_s = [[] for _ in range(2)]
    hp = _prenorm(xp, norm_pre, 0, tm_o)
    hs = _prenorm(xs, norm_pre, 0, tm_s)
    for l in range(depth):
        kv_p, misc, lo = _project(hp, w_steps, tab_p, tm_p, l, depth, kv_p, BF16)
        o_sb = _sb_prompt(kv_p, lo, l, b, t)
        o_hg, hg_new = _hg_prompt(misc, lo, hgrn_lb_logits, hgrn_out_norm, l, b, t)
        o_cv, cv_new = _conv_prompt(lo, conv_w, l, b, t)
        o_mb = _mb_prompt(kv_p, misc, lo, l, b, t)
        xp, hp = _out_project((o_sb, o_hg, o_cv, o_mb), w_out_b, xp, norm_post, norm_pre, l, tm_o)
        states_p[0].append(hg_new)
        states_p[1].append(cv_new)

        kv_s, misc_s, lo_s = _project(hs, w_steps, tab_s, tm_s, l, depth, kv_s, F32)
        kv4 = [a.reshape(depth, bd, td * N_HEADS, HEAD_DIM) for a in kv_s]
        misc4 = misc_s.reshape(N_MISC, bd, td, GROUP_WIDTH)
        lo4 = lo_s.reshape(N_LO, bd, td, GROUP_WIDTH)
        s_sb = _sb_decode(kv4, lo4, c_sb_k, c_sb_v, page_table, l)
        s_hg, hg_new_s, s_cv, cv_new_s = _small_decode(misc4, lo4, state_hgrn, state_conv, hgrn_lb_logits,
                                                       hgrn_out_norm, conv_w, l)
        s_mb = _mb_decode(kv4, misc4, lo4, c_mb_k, c_mb_v, page_table, l)
        mixed_s = tuple(a.reshape(ms, GROUP_WIDTH) for a in (s_sb, s_hg, s_cv, s_mb))
        xs, hs = _out_project(mixed_s, w_out_b, xs, norm_post, norm_pre, l, tm_s)
        states_s[0].append(hg_new_s)
        states_s[1].append(cv_new_s)

    heads_p = lambda a: a.reshape(depth, b, t, N_HEADS, HEAD_DIM)
    heads_s = lambda a: a.reshape(depth, bd, td, N_HEADS, HEAD_DIM)
    return (xp.reshape(b, t, d), xs.reshape(bd, td, d),
            *(heads_p(a) for a in kv_p), jnp.stack(states_p[0]), jnp.stack(states_p[1]),
            *(heads_s(a) for a in kv_s), jnp.stack(states_s[0]), jnp.stack(states_s[1]))
```
